```python
import math
import jax
import jax.numpy as jnp
from jax import lax
import numpy as np

D_MODEL = 2048
BATCH = 4
SEQ = 4096
DEPTH = 4

CHUNK = 64
N_A = DEPTH // 2
N_B = DEPTH - N_A
ALPHA = (2.0 * DEPTH) ** 0.25
BETA = (8.0 * DEPTH) ** -0.25
LN_EPS = 1e-5
RET_HEADS = 8
RET_DK = D_MODEL // RET_HEADS
RET_DV = 2 * D_MODEL // RET_HEADS
RET_QK = RET_HEADS * RET_DK
RET_V = RET_HEADS * RET_DV
RET_IN = 2 * RET_QK + 2 * RET_V
ROPE_BASE = 10000.0
SB_HEADS = 16
SB_DH = D_MODEL // SB_HEADS
Q_BLOCK = 128
N_KEYS = 128
N_EXPERTS = N_KEYS * N_KEYS
PEER_HEADS = 8
PEER_DQ = 256
PEER_TOPK = 16
PEER_BLOCK = 128

kernel_name = "yoco_retention_stickbreaking_peer_deepnorm"


def layer_norm(x, g, b):
    xf = x.astype(jnp.float32)
    mu = jnp.mean(xf, axis=-1, keepdims=True)
    var = jnp.mean(jnp.square(xf - mu), axis=-1, keepdims=True)
    y = (xf - mu) * lax.rsqrt(var + LN_EPS)
    return (y * g.astype(jnp.float32) + b.astype(jnp.float32)).astype(x.dtype)


def rope(x, cos, sin):
    x1, x2 = jnp.split(x, 2, axis=-1)
    c = cos[None, :, None, :]
    s = sin[None, :, None, :]
    return jnp.concatenate([x1 * c - x2 * s, x1 * s + x2 * c], axis=-1)


def retention(x, w_in, gn_g, w_out):
    B, S, _ = x.shape
    nc = S // CHUNK
    proj = x @ w_in
    q, k, v, g = jnp.split(proj, [RET_QK, 2 * RET_QK, 2 * RET_QK + RET_V], axis=-1)
    pos = jnp.arange(S, dtype=jnp.float32)
    inv_freq = ROPE_BASE ** (-jnp.arange(0, RET_DK, 2, dtype=jnp.float32) / RET_DK)
    ang = pos[:, None] * inv_freq[None, :]
    cos, sin = jnp.cos(ang), jnp.sin(ang)
    q = rope(q.reshape(B, S, RET_HEADS, RET_DK).astype(jnp.float32), cos, sin) * (RET_DK ** -0.5)
    k = rope(k.reshape(B, S, RET_HEADS, RET_DK).astype(jnp.float32), cos, sin)
    v = v.reshape(B, S, RET_HEADS, RET_DV).astype(jnp.float32)
    qc = q.transpose(0, 2, 1, 3).reshape(B, RET_HEADS, nc, CHUNK, RET_DK)
    kc = k.transpose(0, 2, 1, 3).reshape(B, RET_HEADS, nc, CHUNK, RET_DK)
    vc = v.transpose(0, 2, 1, 3).reshape(B, RET_HEADS, nc, CHUNK, RET_DV)
    log_gamma = jnp.log(1.0 - jnp.exp2(-5.0 - jnp.arange(RET_HEADS, dtype=jnp.float32)))
    n = jnp.arange(CHUNK, dtype=jnp.float32)
    dist = jnp.abs(n[:, None] - n[None, :])
    decay_intra = jnp.exp(log_gamma[:, None, None] * dist)
    scores = jnp.einsum('bhcnd,bhcmd->bhcnm', qc, kc) * decay_intra[None, :, None]
    o_intra = jnp.einsum('bhcnm,bhcme->bhcne', scores, vc)
    q_dec = jnp.exp(log_gamma[:, None] * (n + 1.0))
    k_dec = jnp.exp(log_gamma[:, None] * (CHUNK - 1.0 - n))
    chunk_dec = jnp.exp(log_gamma * CHUNK)

    def step(state, inp):
        qi, ki, vi = inp
        o = jnp.einsum('bhnd,bhde->bhne', qi * q_dec[None, :, :, None], state)
        state = state * chunk_dec[None, :, None, None] + jnp.einsum(
            'bhmd,bhme->bhde', ki * k_dec[None, :, :, None], vi)
        return state, o

    state0 = jnp.zeros((B, RET_HEADS, RET_DK, RET_DV), jnp.float32)
    xs = (qc.transpose(2, 0, 1, 3, 4), kc.transpose(2, 0, 1, 3, 4), vc.transpose(2, 0, 1, 3, 4))
    _, o_inter = lax.scan(step, state0, xs)
    o = o_intra + o_inter.transpose(1, 2, 0, 3, 4)
    o = o.reshape(B, RET_HEADS, S, RET_DV).transpose(0, 2, 1, 3)
    mu = jnp.mean(o, axis=-1, keepdims=True)
    var = jnp.mean(jnp.square(o - mu), axis=-1, keepdims=True)
    o = ((o - mu) * lax.rsqrt(var + LN_EPS)).reshape(B, S, RET_V) * gn_g.astype(jnp.float32)
    out = jax.nn.silu(g.astype(jnp.float32)) * o
    return out.astype(x.dtype) @ w_out


def shared_kv(x, kv_w):
    B, S, _ = x.shape
    kv = x @ kv_w
    k, v = jnp.split(kv, 2, axis=-1)
    k = k.reshape(B, S, SB_HEADS, SB_DH).transpose(0, 2, 1, 3)
    v = v.reshape(B, S, SB_HEADS, SB_DH).transpose(0, 2, 1, 3)
    return k, v


def stick_breaking(x, wq, w_out, k, v):
    B, S, _ = x.shape
    q = (x @ wq).reshape(B, S, SB_HEADS, SB_DH).transpose(0, 2, 1, 3)
    scale = SB_DH ** -0.5
    outs = []
    for blk in range(S // Q_BLOCK):
        t0 = blk * Q_BLOCK
        L = t0 + Q_BLOCK
        qb = q[:, :, t0:L]
        kb = k[:, :, :L]
        vb = v[:, :, :L]
        z = jnp.einsum('bhtd,bhsd->bhts', qb, kb).astype(jnp.float32) * scale
        t_idx = t0 + jnp.arange(Q_BLOCK)[:, None]
        s_idx = jnp.arange(L)[None, :]
        causal = s_idx < t_idx
        log_beta = jax.nn.log_sigmoid(z)
        log_1m = jnp.where(causal, jax.nn.log_sigmoid(-z), 0.0)
        tail = lax.cumsum(log_1m, axis=3, reverse=True) - log_1m
        w = jnp.where(causal, jnp.exp(log_beta + tail), 0.0)
        outs.append(jnp.einsum('bhts,bhsd->bhtd', w.astype(vb.dtype), vb))
    o = jnp.concatenate(outs, axis=2)
    o = o.transpose(0, 2, 1, 3).reshape(B, S, D_MODEL)
    return o @ w_out


def peer(x, wq, sub_keys, u_tab, v_tab):
    B, S, D = x.shape
    T = B * S
    xt = x.reshape(T, D)
    q = (xt @ wq).reshape(T, PEER_HEADS, 2, PEER_DQ // 2)
    s = jnp.einsum('thpd,hpkd->thpk', q, sub_keys).astype(jnp.float32)
    v_top, i_top = lax.top_k(s, PEER_TOPK)
    cand = (v_top[:, :, 0, :, None] + v_top[:, :, 1, None, :]).reshape(T, PEER_HEADS, PEER_TOPK * PEER_TOPK)
    cid = (i_top[:, :, 0, :, None] * N_KEYS + i_top[:, :, 1, None, :]).reshape(T, PEER_HEADS, PEER_TOPK * PEER_TOPK)
    sc, pos = lax.top_k(cand, PEER_TOPK)
    ids = jnp.take_along_axis(cid, pos, axis=-1)
    g = jax.nn.softmax(sc, axis=-1)
    nblk = T // PEER_BLOCK

    def block(args):
        xb, idb, gb = args
        ub = u_tab[idb]
        h = jax.nn.gelu(jnp.einsum('td,thkd->thk', xb, ub).astype(jnp.float32))
        return jnp.einsum('thk,thkd->td', (gb * h).astype(xb.dtype), v_tab[idb])

    y = lax.map(block, (xt.reshape(nblk, PEER_BLOCK, D),
                        ids.reshape(nblk, PEER_BLOCK, PEER_HEADS, PEER_TOPK),
                        g.reshape(nblk, PEER_BLOCK, PEER_HEADS, PEER_TOPK)))
    return y.reshape(B, S, D)


def setup_inputs(seed: int = 0) -> dict:
    key = jax.random.key(seed)
    ks = jax.random.split(key, 16)
    D = D_MODEL
    nrm = jax.random.normal
    x = nrm(ks[0], (BATCH, SEQ, D), jnp.float32)
    col_scale = jnp.concatenate([jnp.ones((2 * RET_QK,), jnp.float32),
                                 jnp.full((RET_V,), BETA, jnp.float32),
                                 jnp.ones((RET_V,), jnp.float32)])
    ret_w_in = nrm(ks[1], (N_A, D, RET_IN), jnp.float32) * (D ** -0.5) * col_scale
    ret_gn_g = 1.0 + 0.02 * nrm(ks[2], (N_A, RET_V), jnp.float32)
    ret_w_out = nrm(ks[3], (N_A, RET_V, D), jnp.float32) * (RET_V ** -0.5) * BETA
    kv_scale = jnp.concatenate([jnp.ones((D,), jnp.float32), jnp.full((D,), BETA, jnp.float32)])
    kv_w = nrm(ks[4], (D, 2 * D), jnp.float32) * (D ** -0.5) * kv_scale
    sb_wq = nrm(ks[5], (N_B, D, D), jnp.float32) * (D ** -0.5)
    sb_w_out = nrm(ks[6], (N_B, D, D), jnp.float32) * (D ** -0.5) * BETA
    peer_wq = nrm(ks[7], (DEPTH, D, PEER_HEADS * PEER_DQ), jnp.float32) * (D ** -0.5)
    peer_sub_keys = nrm(ks[8], (DEPTH, PEER_HEADS, 2, N_KEYS, PEER_DQ // 2), jnp.float32) * ((PEER_DQ // 2) ** -0.5)
    peer_u = nrm(ks[9], (DEPTH, N_EXPERTS, D), jnp.float32) * (D ** -0.5) * BETA
    peer_v = nrm(ks[10], (DEPTH, N_EXPERTS, D), jnp.float32) * BETA
    ln_g = 1.0 + 0.02 * nrm(ks[11], (DEPTH, 2, D), jnp.float32)
    ln_b = 0.02 * nrm(ks[12], (DEPTH, 2, D), jnp.float32)
    return {"x": x, "ret_w_in": ret_w_in, "ret_gn_g": ret_gn_g, "ret_w_out": ret_w_out,
            "kv_w": kv_w, "sb_wq": sb_wq, "sb_w_out": sb_w_out,
            "peer_wq": peer_wq, "peer_sub_keys": peer_sub_keys, "peer_u": peer_u, "peer_v": peer_v,
            "ln_g": ln_g, "ln_b": ln_b}


def reference(x, ret_w_in, ret_gn_g, ret_w_out, kv_w, sb_wq, sb_w_out,
              peer_wq, peer_sub_keys, peer_u, peer_v, ln_g, ln_b):
    k_sh = None
    v_sh = None
    for l in range(DEPTH):
        if l < N_A:
            mix = retention(x, ret_w_in[l], ret_gn_g[l], ret_w_out[l])
        else:
            mix = stick_breaking(x, sb_wq[l - N_A], sb_w_out[l - N_A], k_sh, v_sh)
        x = layer_norm(ALPHA * x + mix, ln_g[l, 0], ln_b[l, 0])
        ffn = peer(x, peer_wq[l], peer_sub_keys[l], peer_u[l], peer_v[l])
        x = layer_norm(ALPHA * x + ffn, ln_g[l, 1], ln_b[l, 1])
        if l == N_A - 1:
            k_sh, v_sh = shared_kv(x, kv_w)
    return x
```

```python
import functools

import jax
import jax.numpy as jnp
from jax import lax
from jax.experimental import pallas as pl
from jax.experimental.pallas import tpu as pltpu

F32 = jnp.float32
BF16 = jnp.bfloat16

LN_EPS = 1e-5
ROPE_BASE = 10000.0
CHUNK = 64
RET_HEADS = 8
SB_HEADS = 16
PEER_HEADS = 8
PEER_TOPK = 16
N_KEYS = 128

MIB = 1024 * 1024
VMEM_LIMIT_BYTES = 56 * MIB

MM_TM, MM_TN = 1024, 512
LN_TM = 256
RET_BLOCK = 256
SB_TQ = SB_TK = 256
PEER_SCORE_TM = 512
PEER_TOPK_TM = 128
PEER_TM = 512
PEER_NI = 4

_CAND = [(p, q) for p in range(PEER_TOPK) for q in range(PEER_TOPK) if (p + 1) * (q + 1) <= PEER_TOPK]
_CAND_ROWS = -(-len(_CAND) // 8) * 8


def _params(*sem):
    return pltpu.CompilerParams(dimension_semantics=sem, vmem_limit_bytes=VMEM_LIMIT_BYTES)


def _mm_kernel(a_ref, b_ref, o_ref):
    o_ref[...] = jnp.dot(a_ref[...], b_ref[...], preferred_element_type=F32).astype(o_ref.dtype)


def _matmul(a, b, out_dtype):
    m, k = a.shape
    n = b.shape[1]
    tm, tn = min(MM_TM, m), min(MM_TN, n)
    return pl.pallas_call(
        _mm_kernel,
        grid=(m // tm, n // tn),
        in_specs=[pl.BlockSpec((tm, k), lambda i, j: (i, 0)),
                  pl.BlockSpec((k, tn), lambda i, j: (0, j))],
        out_specs=pl.BlockSpec((tm, tn), lambda i, j: (i, j)),
        out_shape=jax.ShapeDtypeStruct((m, n), out_dtype),
        compiler_params=_params("parallel", "parallel"),
        name="mm",
    )(a, b)


def _ln_kernel(x_ref, y_ref, g_ref, b_ref, of_ref, ob_ref, *, alpha):
    z = alpha * x_ref[...] + y_ref[...].astype(F32)
    mu = jnp.mean(z, axis=-1, keepdims=True)
    d = z - mu
    var = jnp.mean(d * d, axis=-1, keepdims=True)
    o = d * lax.rsqrt(var + LN_EPS) * g_ref[...] + b_ref[...]
    of_ref[...] = o
    ob_ref[...] = o.astype(BF16)


def _ln_res(x, y, g, b, alpha):
    t, d = x.shape
    tm = min(LN_TM, t)
    row = pl.BlockSpec((tm, d), lambda i: (i, 0))
    vec = pl.BlockSpec((1, d), lambda i: (0, 0))
    return pl.pallas_call(
        functools.partial(_ln_kernel, alpha=alpha),
        grid=(t // tm,),
        in_specs=[row, row, vec, vec],
        out_specs=[row, row],
        out_shape=[jax.ShapeDtypeStruct((t, d), F32), jax.ShapeDtypeStruct((t, d), BF16)],
        compiler_params=_params("parallel"),
        name="ln_res",
    )(x, y, g.reshape(1, d), b.reshape(1, d))


def _rope(x, cos, sin):
    half = x.shape[1] // 2
    x1, x2 = x[:, :half], x[:, half:]
    return jnp.concatenate([x1 * cos - x2 * sin, x1 * sin + x2 * cos], axis=1)


def _ret_kernel(dec_ref, q_ref, k_ref, v_ref, g_ref, cos_ref, sin_ref, dmat_ref, qdec_ref, kdec_ref, gn_ref,
                o_ref, state_ref, *, scale):
    h = pl.program_id(1)

    @pl.when(pl.program_id(2) == 0)
    def _():
        state_ref[...] = jnp.zeros_like(state_ref)

    cos, sin = cos_ref[...], sin_ref[...]
    qr = _rope(q_ref[...].astype(F32), cos, sin) * scale
    kr = _rope(k_ref[...].astype(F32), cos, sin)
    v = v_ref[...]
    sc = lax.dot_general(qr.astype(BF16), kr.astype(BF16), (((1,), (1,)), ((), ())),
                         preferred_element_type=F32) * dmat_ref[0]
    o = jnp.dot(sc.astype(BF16), v, preferred_element_type=F32)
    o = o + jnp.dot((qr * qdec_ref[0]).astype(BF16), state_ref[...].astype(BF16), preferred_element_type=F32)
    kd = (kr * kdec_ref[0]).astype(BF16)
    state_ref[...] = state_ref[...] * dec_ref[h] + lax.dot_general(
        kd, v, (((0,), (0,)), ((), ())), preferred_element_type=F32)
    mu = jnp.mean(o, axis=-1, keepdims=True)
    d = o - mu
    var = jnp.mean(d * d, axis=-1, keepdims=True)
    on = d * lax.rsqrt(var + LN_EPS) * gn_ref[...]
    g = g_ref[...].astype(F32)
    o_ref[...] = (g * jax.nn.sigmoid(g) * on).astype(BF16)


def _retention_tables(seq, dk, blk):
    pos = jnp.arange(seq, dtype=F32)
    inv_freq = ROPE_BASE ** (-jnp.arange(0, dk, 2, dtype=F32) / dk)
    ang = pos[:, None] * inv_freq[None, :]
    log_gamma = jnp.log(1.0 - jnp.exp2(-5.0 - jnp.arange(RET_HEADS, dtype=F32)))
    n = jnp.arange(blk, dtype=F32)
    dist = jnp.abs(n[:, None] - n[None, :])
    chunk = jnp.arange(blk) // CHUNK
    visible = (chunk[None, :] <= chunk[:, None]).astype(F32)
    dmat = jnp.exp(log_gamma[:, None, None] * dist) * visible
    qdec = jnp.broadcast_to(jnp.exp(log_gamma[:, None] * (n + 1.0))[:, :, None], (RET_HEADS, blk, dk))
    kdec = jnp.broadcast_to(jnp.exp(log_gamma[:, None] * (blk - 1.0 - n))[:, :, None], (RET_HEADS, blk, dk))
    dec = jnp.exp(log_gamma * blk)
    return jnp.cos(ang), jnp.sin(ang), dmat, qdec, kdec, dec


def _retention_core(proj, gn_g, batch, seq, d_model):
    dk = d_model // RET_HEADS
    dv = 2 * d_model // RET_HEADS
    blk = min(RET_BLOCK, seq)
    nblk = seq // blk
    cos, sin, dmat, qdec, kdec, dec = _retention_tables(seq, dk, blk)
    k_col0 = RET_HEADS
    v_col0 = 2 * RET_HEADS * dk // dv
    g_col0 = v_col0 + RET_HEADS
    head_tab = lambda b, h, s, dec: (h, 0, 0)
    grid_spec = pltpu.PrefetchScalarGridSpec(
        num_scalar_prefetch=1,
        grid=(batch, RET_HEADS, nblk),
        in_specs=[
            pl.BlockSpec((blk, dk), lambda b, h, s, dec: (b * nblk + s, h)),
            pl.BlockSpec((blk, dk), lambda b, h, s, dec: (b * nblk + s, k_col0 + h)),
            pl.BlockSpec((blk, dv), lambda b, h, s, dec: (b * nblk + s, v_col0 + h)),
            pl.BlockSpec((blk, dv), lambda b, h, s, dec: (b * nblk + s, g_col0 + h)),
            pl.BlockSpec((blk, dk // 2), lambda b, h, s, dec: (s, 0)),
            pl.BlockSpec((blk, dk // 2), lambda b, h, s, dec: (s, 0)),
            pl.BlockSpec((1, blk, blk), head_tab),
            pl.BlockSpec((1, blk, dk), head_tab),
            pl.BlockSpec((1, blk, dk), head_tab),
            pl.BlockSpec((1, dv), lambda b, h, s, dec: (0, h)),
        ],
        out_specs=pl.BlockSpec((blk, dv), lambda b, h, s, dec: (b * nblk + s, h)),
        scratch_shapes=[pltpu.VMEM((dk, dv), F32)],
    )
    return pl.pallas_call(
        functools.partial(_ret_kernel, scale=dk ** -0.5),
        grid_spec=grid_spec,
        out_shape=jax.ShapeDtypeStruct((batch * seq, RET_HEADS * dv), BF16),
        compiler_params=_params("parallel", "parallel", "arbitrary"),
        name="retention",
    )(dec, proj, proj, proj, proj, cos, sin, dmat, qdec, kdec, gn_g.reshape(1, -1))


def _sb_kernel(q_ref, k_ref, v_ref, o_ref, *, scale):
    tq, dh = q_ref.shape
    tk = min(SB_TK, k_ref.shape[0])
    qi = pl.program_id(2)
    q = q_ref[...]
    t_idx = qi * tq + lax.broadcasted_iota(jnp.int32, (tq, tk), 0)
    s_loc = lax.broadcasted_iota(jnp.int32, (tq, tk), 1)
    later = (lax.broadcasted_iota(jnp.int32, (tk, tk), 0) > lax.broadcasted_iota(jnp.int32, (tk, tk), 1)).astype(BF16)
    n_kb = (qi + 1) * tq // tk

    def body(it, carry):
        c, acc = carry
        kb = n_kb - 1 - it
        off = pl.multiple_of(kb * tk, tk)
        kblk = k_ref[pl.ds(off, tk), :]
        vblk = v_ref[pl.ds(off, tk), :]
        z = lax.dot_general(q, kblk, (((1,), (1,)), ((), ())), preferred_element_type=F32) * scale
        causal = (kb * tk + s_loc) < t_idx
        log_beta = jnp.minimum(z, 0.0) - jnp.log1p(jnp.exp(-jnp.abs(z)))
        log_1m = jnp.where(causal, log_beta - z, 0.0)
        hi = log_1m.astype(BF16)
        lo = (log_1m - hi.astype(F32)).astype(BF16)
        tail = (jnp.dot(hi, later, preferred_element_type=F32)
                + jnp.dot(lo, later, preferred_element_type=F32) + c)
        w = jnp.where(causal, jnp.exp(log_beta + tail), 0.0)
        acc = acc + jnp.dot(w.astype(BF16), vblk, preferred_element_type=F32)
        c = c + jnp.sum(log_1m, axis=1, keepdims=True)
        return c, acc

    _, acc = lax.fori_loop(0, n_kb, body, (jnp.zeros((tq, 1), F32), jnp.zeros((tq, dh), F32)))
    o_ref[...] = acc.astype(o_ref.dtype)


def _stick_breaking_core(q, kv, batch, seq, d_model):
    dh = d_model // SB_HEADS
    tq = min(SB_TQ, seq)
    nq = seq // tq
    return pl.pallas_call(
        functools.partial(_sb_kernel, scale=dh ** -0.5),
        grid=(batch, SB_HEADS, nq),
        in_specs=[pl.BlockSpec((tq, dh), lambda b, h, i: (b * nq + i, h)),
                  pl.BlockSpec((seq, dh), lambda b, h, i: (b, h)),
                  pl.BlockSpec((seq, dh), lambda b, h, i: (b, SB_HEADS + h))],
        out_specs=pl.BlockSpec((tq, dh), lambda b, h, i: (b * nq + i, h)),
        out_shape=jax.ShapeDtypeStruct((batch * seq, d_model), BF16),
        compiler_params=_params("parallel", "parallel", "parallel"),
        name="stick_breaking",
    )(q, kv, kv)


def _peer_score_kernel(xT_ref, wqT_ref, keys_ref, sT_ref):
    qT = jnp.dot(wqT_ref[...], xT_ref[...], preferred_element_type=F32)
    dq = keys_ref.shape[2]
    for g in range(keys_ref.shape[0]):
        sT_ref[g] = jnp.dot(keys_ref[g], qT[g * dq:(g + 1) * dq].astype(BF16), preferred_element_type=F32)


def _peer_scores(xT, wqT, keys):
    d, t = xT.shape
    groups, nk, dq = keys.shape
    tm = min(PEER_SCORE_TM, t)
    return pl.pallas_call(
        _peer_score_kernel,
        grid=(t // tm,),
        in_specs=[pl.BlockSpec((d, tm), lambda i: (0, i)),
                  pl.BlockSpec(wqT.shape, lambda i: (0, 0)),
                  pl.BlockSpec(keys.shape, lambda i: (0, 0, 0))],
        out_specs=pl.BlockSpec((groups, nk, tm), lambda i: (0, 0, i)),
        out_shape=jax.ShapeDtypeStruct((groups, nk, t), F32),
        compiler_params=_params("parallel"),
        name="peer_scores",
    )(xT, wqT, keys)


def _first_argmax(s, row):
    m = jnp.max(s, axis=0, keepdims=True)
    idx = jnp.min(jnp.where(s == m, row, s.shape[0]), axis=0, keepdims=True)
    return m, row == idx


def _topk_rank(s, vals_ref):
    row = lax.broadcasted_iota(jnp.int32, s.shape, 0)
    rank = jnp.full(s.shape, float(PEER_TOPK), F32)
    for k in range(PEER_TOPK):
        m, sel = _first_argmax(s, row)
        rank = jnp.where(sel, float(k), rank)
        s = jnp.where(sel, -jnp.inf, s)
        vals_ref[k:k + 1, :] = m
    return rank


def _peer_topk_kernel(sT_ref, r2_ref, e2_ref, n_ref, a_ref, v1_ref, v2_ref, c_ref, sel_ref):
    tm = sT_ref.shape[2]
    crow = lax.broadcasted_iota(jnp.int32, (_CAND_ROWS, tm), 0)

    def head(h, carry):
        s1 = sT_ref[2 * h]
        s2 = sT_ref[2 * h + 1]
        rank1 = _topk_rank(s1, v1_ref)
        rank2 = _topk_rank(s2, v2_ref)
        c_ref[...] = jnp.full(c_ref.shape, -jnp.inf, F32)
        for r, (p, q) in enumerate(_CAND):
            c_ref[r:r + 1, :] = v1_ref[p:p + 1, :] + v2_ref[q:q + 1, :]
        c = c_ref[...]
        chosen = jnp.zeros(c.shape, F32)
        m0 = v1_ref[0:1, :] + v2_ref[0:1, :]
        z = jnp.zeros((1, tm), F32)
        for _ in range(PEER_TOPK):
            m, sel = _first_argmax(c, crow)
            chosen = jnp.where(sel, 1.0, chosen)
            c = jnp.where(sel, -jnp.inf, c)
            z = z + jnp.exp(m - m0)
        sel_ref[...] = chosen
        n = jnp.zeros(s1.shape, F32)
        r0 = 0
        for p in range(PEER_TOPK):
            width = sum(1 for (pp, _) in _CAND if pp == p)
            n_p = jnp.sum(sel_ref[r0:r0 + width, :], axis=0, keepdims=True)
            n = jnp.where(rank1 == float(p), n_p, n)
            r0 += width
        r2_ref[h] = rank2
        e2_ref[h] = jnp.exp(s2 - v2_ref[0:1, :])
        n_ref[h] = n
        a_ref[h] = jnp.exp(s1 - v1_ref[0:1, :]) / z
        return carry

    lax.fori_loop(0, r2_ref.shape[0], head, 0)


def _peer_topk(sT):
    groups, nk, t = sT.shape
    heads = groups // 2
    tm = min(PEER_TOPK_TM, t)
    out = pl.BlockSpec((heads, nk, tm), lambda i: (0, 0, i))
    shape = jax.ShapeDtypeStruct((heads, nk, t), F32)
    return pl.pallas_call(
        _peer_topk_kernel,
        grid=(t // tm,),
        in_specs=[pl.BlockSpec((groups, nk, tm), lambda i: (0, 0, i))],
        out_specs=[out, out, out, out],
        out_shape=[shape, shape, shape, shape],
        scratch_shapes=[pltpu.VMEM((PEER_TOPK, tm), F32), pltpu.VMEM((PEER_TOPK, tm), F32),
                        pltpu.VMEM((_CAND_ROWS, tm), F32), pltpu.VMEM((_CAND_ROWS, tm), F32)],
        compiler_params=_params("parallel"),
        name="peer_topk",
    )(sT)


def _peer_dense_kernel(xT_ref, u_ref, vT_ref, r2_ref, e2_ref, n_ref, a_ref, y_ref, act_ref):
    e = pl.program_id(1)
    heads, nk, _ = r2_ref.shape
    ni = u_ref.shape[0] // nk

    @pl.when(e == 0)
    def _():
        y_ref[...] = jnp.zeros_like(y_ref)

    hT = jnp.dot(u_ref[...], xT_ref[...], preferred_element_type=F32)
    for ii in range(ni):
        i = e * ni + ii
        w = jnp.zeros((nk, hT.shape[1]), F32)
        for h in range(heads):
            n = n_ref[h, pl.ds(i, 1), :]
            a = a_ref[h, pl.ds(i, 1), :]
            w = w + jnp.where(r2_ref[h] < n, e2_ref[h], 0.0) * a
        act_ref[ii * nk:(ii + 1) * nk, :] = (jax.nn.gelu(hT[ii * nk:(ii + 1) * nk]) * w).astype(BF16)
    y_ref[...] += jnp.dot(vT_ref[...], act_ref[...], preferred_element_type=F32)


def _peer_dense(xT, u, vT, r2, e2, n, a):
    d, t = xT.shape
    n_exp = u.shape[0]
    heads, nk, _ = r2.shape
    tm = min(PEER_TM, t)
    te = PEER_NI * nk
    sel = pl.BlockSpec((heads, nk, tm), lambda i, e: (0, 0, i))
    return pl.pallas_call(
        _peer_dense_kernel,
        grid=(t // tm, n_exp // te),
        in_specs=[pl.BlockSpec((d, tm), lambda i, e: (0, i)),
                  pl.BlockSpec((te, d), lambda i, e: (e, 0)),
                  pl.BlockSpec((d, te), lambda i, e: (0, e)),
                  sel, sel, sel, sel],
        out_specs=pl.BlockSpec((d, tm), lambda i, e: (0, i)),
        out_shape=jax.ShapeDtypeStruct((d, t), F32),
        scratch_shapes=[pltpu.VMEM((te, tm), BF16)],
        compiler_params=_params("parallel", "arbitrary"),
        name="peer_dense",
    )(xT, u, vT, r2, e2, n, a)


def _peer(xb, wq, sub_keys, u_tab, v_tab):
    xT = xb.T
    keys = sub_keys.reshape(-1, sub_keys.shape[-2], sub_keys.shape[-1]).astype(BF16)
    sT = _peer_scores(xT, wq.T.astype(BF16), keys)
    r2, e2, n, a = _peer_topk(sT)
    return _peer_dense(xT, u_tab.astype(BF16), v_tab.T.astype(BF16), r2, e2, n, a)


def kernel(x, ret_w_in, ret_gn_g, ret_w_out, kv_w, sb_wq, sb_w_out, peer_wq, peer_sub_keys, peer_u, peer_v,
           ln_g, ln_b):
    batch, seq, d_model = x.shape
    depth = ln_g.shape[0]
    n_a = ret_w_in.shape[0]
    alpha = (2.0 * depth) ** 0.25
    xf = x.reshape(batch * seq, d_model)
    xb = xf.astype(BF16)
    kv = None
    for l in range(depth):
        if l < n_a:
            proj = _matmul(xb, ret_w_in[l].astype(BF16), BF16)
            o = _retention_core(proj, ret_gn_g[l], batch, seq, d_model)
            mix = _matmul(o, ret_w_out[l].astype(BF16), F32)
        else:
            q = _matmul(xb, sb_wq[l - n_a].astype(BF16), BF16)
            o = _stick_breaking_core(q, kv, batch, seq, d_model)
            mix = _matmul(o, sb_w_out[l - n_a].astype(BF16), F32)
        xf, xb = _ln_res(xf, mix, ln_g[l, 0], ln_b[l, 0], alpha)
        yT = _peer(xb, peer_wq[l], peer_sub_keys[l], peer_u[l], peer_v[l])
        xf, xb = _ln_res(xf, yT.T, ln_g[l, 1], ln_b[l, 1], alpha)
        if l == n_a - 1:
            kv = _matmul(xb, kv_w.astype(BF16), BF16)
    return xf.reshape(batch, seq, d_model)
```

```python
import functools

import jax
import jax.numpy as jnp
from jax import lax
from jax.experimental import pallas as pl
from jax.experimental.pallas import tpu as pltpu

F32 = jnp.float32
BF16 = jnp.bfloat16

LN_EPS = 1e-5
ROPE_BASE = 10000.0
CHUNK = 64
RET_HEADS = 8
SB_HEADS = 16
PEER_HEADS = 8
PEER_TOPK = 16
N_KEYS = 128

MIB = 1024 * 1024
VMEM_LIMIT_BYTES = 56 * MIB

MM_TM, MM_TN = 1024, 512
LN_TM = 256
RET_BLOCK = 256
SB_TQ = SB_TK = 256
SB_EXP_ZERO = -104.0
PEER_SCORE_TM = 512
PEER_TOPK_TM = 128
PEER_TM = 512
PEER_NI = 8
PEER_GROUPS = 2

_CAND = [(p, q) for p in range(PEER_TOPK) for q in range(PEER_TOPK) if (p + 1) * (q + 1) <= PEER_TOPK]
_CAND_ROWS = -(-len(_CAND) // 8) * 8


def _params(*sem):
    return pltpu.CompilerParams(dimension_semantics=sem, vmem_limit_bytes=VMEM_LIMIT_BYTES)


def _mm_kernel(a_ref, b_ref, o_ref):
    o_ref[...] = jnp.dot(a_ref[...], b_ref[...], preferred_element_type=F32).astype(o_ref.dtype)


def _matmul(a, b, out_dtype):
    m, k = a.shape
    n = b.shape[1]
    tm, tn = min(MM_TM, m), min(MM_TN, n)
    return pl.pallas_call(
        _mm_kernel,
        grid=(m // tm, n // tn),
        in_specs=[pl.BlockSpec((tm, k), lambda i, j: (i, 0)),
                  pl.BlockSpec((k, tn), lambda i, j: (0, j))],
        out_specs=pl.BlockSpec((tm, tn), lambda i, j: (i, j)),
        out_shape=jax.ShapeDtypeStruct((m, n), out_dtype),
        compiler_params=_params("parallel", "parallel"),
        name="mm",
    )(a, b)


def _ln_kernel(x_ref, y_ref, g_ref, b_ref, of_ref, ob_ref, *, alpha):
    z = alpha * x_ref[...] + y_ref[...].astype(F32)
    mu = jnp.mean(z, axis=-1, keepdims=True)
    d = z - mu
    var = jnp.mean(d * d, axis=-1, keepdims=True)
    o = d * lax.rsqrt(var + LN_EPS) * g_ref[...] + b_ref[...]
    of_ref[...] = o
    ob_ref[...] = o.astype(BF16)


def _ln_res(x, y, g, b, alpha):
    t, d = x.shape
    tm = min(LN_TM, t)
    row = pl.BlockSpec((tm, d), lambda i: (i, 0))
    vec = pl.BlockSpec((1, d), lambda i: (0, 0))
    return pl.pallas_call(
        functools.partial(_ln_kernel, alpha=alpha),
        grid=(t // tm,),
        in_specs=[row, row, vec, vec],
        out_specs=[row, row],
        out_shape=[jax.ShapeDtypeStruct((t, d), F32), jax.ShapeDtypeStruct((t, d), BF16)],
        compiler_params=_params("parallel"),
        name="ln_res",
    )(x, y, g.reshape(1, d), b.reshape(1, d))


def _rope(x, cos, sin):
    half = x.shape[1] // 2
    x1, x2 = x[:, :half], x[:, half:]
    return jnp.concatenate([x1 * cos - x2 * sin, x1 * sin + x2 * cos], axis=1)


def _ret_kernel(dec_ref, q_ref, k_ref, v_ref, g_ref, cos_ref, sin_ref, dmat_ref, qdec_ref, kdec_ref, gn_ref,
                o_ref, state_ref, *, scale):
    h = pl.program_id(1)

    @pl.when(pl.program_id(2) == 0)
    def _():
        state_ref[...] = jnp.zeros_like(state_ref)

    cos, sin = cos_ref[...], sin_ref[...]
    qr = _rope(q_ref[...].astype(F32), cos, sin) * scale
    kr = _rope(k_ref[...].astype(F32), cos, sin)
    v = v_ref[...]
    sc = lax.dot_general(qr.astype(BF16), kr.astype(BF16), (((1,), (1,)), ((), ())),
                         preferred_element_type=F32) * dmat_ref[0]
    o = jnp.dot(sc.astype(BF16), v, preferred_element_type=F32)
    o = o + jnp.dot((qr * qdec_ref[0]).astype(BF16), state_ref[...].astype(BF16), preferred_element_type=F32)
    kd = (kr * kdec_ref[0]).astype(BF16)
    state_ref[...] = state_ref[...] * dec_ref[h] + lax.dot_general(
        kd, v, (((0,), (0,)), ((), ())), preferred_element_type=F32)
    mu = jnp.mean(o, axis=-1, keepdims=True)
    d = o - mu
    var = jnp.mean(d * d, axis=-1, keepdims=True)
    on = d * lax.rsqrt(var + LN_EPS) * gn_ref[...]
    g = g_ref[...].astype(F32)
    o_ref[...] = (g * jax.nn.sigmoid(g) * on).astype(BF16)


def _retention_tables(seq, dk, blk):
    pos = jnp.arange(seq, dtype=F32)
    inv_freq = ROPE_BASE ** (-jnp.arange(0, dk, 2, dtype=F32) / dk)
    ang = pos[:, None] * inv_freq[None, :]
    log_gamma = jnp.log(1.0 - jnp.exp2(-5.0 - jnp.arange(RET_HEADS, dtype=F32)))
    n = jnp.arange(blk, dtype=F32)
    dist = jnp.abs(n[:, None] - n[None, :])
    chunk = jnp.arange(blk) // CHUNK
    visible = (chunk[None, :] <= chunk[:, None]).astype(F32)
    dmat = jnp.exp(log_gamma[:, None, None] * dist) * visible
    qdec = jnp.broadcast_to(jnp.exp(log_gamma[:, None] * (n + 1.0))[:, :, None], (RET_HEADS, blk, dk))
    kdec = jnp.broadcast_to(jnp.exp(log_gamma[:, None] * (blk - 1.0 - n))[:, :, None], (RET_HEADS, blk, dk))
    dec = jnp.exp(log_gamma * blk)
    return jnp.cos(ang), jnp.sin(ang), dmat, qdec, kdec, dec


def _retention_core(proj, gn_g, batch, seq, d_model):
    dk = d_model // RET_HEADS
    dv = 2 * d_model // RET_HEADS
    blk = min(RET_BLOCK, seq)
    nblk = seq // blk
    cos, sin, dmat, qdec, kdec, dec = _retention_tables(seq, dk, blk)
    k_col0 = RET_HEADS
    v_col0 = 2 * RET_HEADS * dk // dv
    g_col0 = v_col0 + RET_HEADS
    head_tab = lambda b, h, s, dec: (h, 0, 0)
    grid_spec = pltpu.PrefetchScalarGridSpec(
        num_scalar_prefetch=1,
        grid=(batch, RET_HEADS, nblk),
        in_specs=[
            pl.BlockSpec((blk, dk), lambda b, h, s, dec: (b * nblk + s, h)),
            pl.BlockSpec((blk, dk), lambda b, h, s, dec: (b * nblk + s, k_col0 + h)),
            pl.BlockSpec((blk, dv), lambda b, h, s, dec: (b * nblk + s, v_col0 + h)),
            pl.BlockSpec((blk, dv), lambda b, h, s, dec: (b * nblk + s, g_col0 + h)),
            pl.BlockSpec((blk, dk // 2), lambda b, h, s, dec: (s, 0)),
            pl.BlockSpec((blk, dk // 2), lambda b, h, s, dec: (s, 0)),
            pl.BlockSpec((1, blk, blk), head_tab),
            pl.BlockSpec((1, blk, dk), head_tab),
            pl.BlockSpec((1, blk, dk), head_tab),
            pl.BlockSpec((1, dv), lambda b, h, s, dec: (0, h)),
        ],
        out_specs=pl.BlockSpec((blk, dv), lambda b, h, s, dec: (b * nblk + s, h)),
        scratch_shapes=[pltpu.VMEM((dk, dv), F32)],
    )
    return pl.pallas_call(
        functools.partial(_ret_kernel, scale=dk ** -0.5),
        grid_spec=grid_spec,
        out_shape=jax.ShapeDtypeStruct((batch * seq, RET_HEADS * dv), BF16),
        compiler_params=_params("parallel", "parallel", "arbitrary"),
        name="retention",
    )(dec, proj, proj, proj, proj, cos, sin, dmat, qdec, kdec, gn_g.reshape(1, -1))


def _sb_kernel(q_ref, k_ref, v_ref, o_ref, *, scale):
    tq, dh = q_ref.shape
    tk = min(SB_TK, k_ref.shape[0])
    qi = pl.program_id(2)
    q = q_ref[...]
    t_idx = qi * tq + lax.broadcasted_iota(jnp.int32, (tq, tk), 0)
    s_loc = lax.broadcasted_iota(jnp.int32, (tq, tk), 1)
    later = (lax.broadcasted_iota(jnp.int32, (tk, tk), 0) > lax.broadcasted_iota(jnp.int32, (tk, tk), 1)).astype(BF16)
    n_kb = (qi + 1) * tq // tk

    def body(carry):
        it, _, c, acc = carry
        kb = n_kb - 1 - it
        off = pl.multiple_of(kb * tk, tk)
        kblk = k_ref[pl.ds(off, tk), :]
        vblk = v_ref[pl.ds(off, tk), :]
        z = lax.dot_general(q, kblk, (((1,), (1,)), ((), ())), preferred_element_type=F32) * scale
        causal = (kb * tk + s_loc) < t_idx
        log_beta = jnp.minimum(z, 0.0) - jnp.log1p(jnp.exp(-jnp.abs(z)))
        log_1m = jnp.where(causal, log_beta - z, 0.0)
        hi = log_1m.astype(BF16)
        lo = (log_1m - hi.astype(F32)).astype(BF16)
        tail = (jnp.dot(hi, later, preferred_element_type=F32)
                + jnp.dot(lo, later, preferred_element_type=F32) + c)
        w = jnp.where(causal, jnp.exp(log_beta + tail), 0.0)
        acc = acc + jnp.dot(w.astype(BF16), vblk, preferred_element_type=F32)
        c = c + jnp.sum(log_1m, axis=1, keepdims=True)
        return it + 1, jnp.max(c), c, acc

    def more(carry):
        it, c_max, _, _ = carry
        return jnp.logical_and(it < n_kb, c_max > SB_EXP_ZERO)

    init = (jnp.int32(0), jnp.float32(0.0), jnp.zeros((tq, 1), F32), jnp.zeros((tq, dh), F32))
    _, _, _, acc = lax.while_loop(more, body, init)
    o_ref[...] = acc.astype(o_ref.dtype)


def _stick_breaking_core(q, kv, batch, seq, d_model):
    dh = d_model // SB_HEADS
    tq = min(SB_TQ, seq)
    nq = seq // tq
    return pl.pallas_call(
        functools.partial(_sb_kernel, scale=dh ** -0.5),
        grid=(batch, SB_HEADS, nq),
        in_specs=[pl.BlockSpec((tq, dh), lambda b, h, i: (b * nq + i, h)),
                  pl.BlockSpec((seq, dh), lambda b, h, i: (b, h)),
                  pl.BlockSpec((seq, dh), lambda b, h, i: (b, SB_HEADS + h))],
        out_specs=pl.BlockSpec((tq, dh), lambda b, h, i: (b * nq + i, h)),
        out_shape=jax.ShapeDtypeStruct((batch * seq, d_model), BF16),
        compiler_params=_params("parallel", "parallel", "parallel"),
        name="stick_breaking",
    )(q, kv, kv)


def _peer_score_kernel(xT_ref, wqT_ref, keys_ref, sT_ref):
    qT = jnp.dot(wqT_ref[...], xT_ref[...], preferred_element_type=F32)
    dq = keys_ref.shape[2]
    for g in range(keys_ref.shape[0]):
        sT_ref[g] = jnp.dot(keys_ref[g], qT[g * dq:(g + 1) * dq].astype(BF16), preferred_element_type=F32)


def _peer_scores(xT, wqT, keys):
    d, t = xT.shape
    groups, nk, dq = keys.shape
    tm = min(PEER_SCORE_TM, t)
    return pl.pallas_call(
        _peer_score_kernel,
        grid=(t // tm,),
        in_specs=[pl.BlockSpec((d, tm), lambda i: (0, i)),
                  pl.BlockSpec(wqT.shape, lambda i: (0, 0)),
                  pl.BlockSpec(keys.shape, lambda i: (0, 0, 0))],
        out_specs=pl.BlockSpec((groups, nk, tm), lambda i: (0, 0, i)),
        out_shape=jax.ShapeDtypeStruct((groups, nk, t), F32),
        compiler_params=_params("parallel"),
        name="peer_scores",
    )(xT, wqT, keys)


def _first_argmax(s, row):
    m = jnp.max(s, axis=0, keepdims=True)
    idx = jnp.min(jnp.where(s == m, row, float(s.shape[0])), axis=0, keepdims=True)
    return m, row == idx


def _row_ids(shape):
    return lax.broadcasted_iota(jnp.int32, shape, 0).astype(F32)


def _topk_rank(s, vals_ref):
    row = _row_ids(s.shape)
    rank = jnp.full(s.shape, float(PEER_TOPK), F32)
    for k in range(PEER_TOPK):
        m, sel = _first_argmax(s, row)
        rank = jnp.where(sel, float(k), rank)
        s = jnp.where(sel, -jnp.inf, s)
        vals_ref[k:k + 1, :] = m
    return rank


def _peer_topk_kernel(sT_ref, r2_ref, e2_ref, n_ref, a_ref, v1_ref, v2_ref, c_ref, sel_ref):
    tm = sT_ref.shape[2]
    crow = _row_ids((_CAND_ROWS, tm))

    def head(h, carry):
        s1 = sT_ref[2 * h]
        s2 = sT_ref[2 * h + 1]
        rank1 = _topk_rank(s1, v1_ref)
        rank2 = _topk_rank(s2, v2_ref)
        c_ref[...] = jnp.full(c_ref.shape, -jnp.inf, F32)
        for r, (p, q) in enumerate(_CAND):
            c_ref[r:r + 1, :] = v1_ref[p:p + 1, :] + v2_ref[q:q + 1, :]
        c = c_ref[...]
        chosen = jnp.zeros(c.shape, F32)
        m0 = v1_ref[0:1, :] + v2_ref[0:1, :]
        z = jnp.zeros((1, tm), F32)
        for _ in range(PEER_TOPK):
            m, sel = _first_argmax(c, crow)
            chosen = jnp.where(sel, 1.0, chosen)
            c = jnp.where(sel, -jnp.inf, c)
            z = z + jnp.exp(m - m0)
        sel_ref[...] = chosen
        n = jnp.zeros(s1.shape, F32)
        r0 = 0
        for p in range(PEER_TOPK):
            width = sum(1 for (pp, _) in _CAND if pp == p)
            n_p = jnp.sum(sel_ref[r0:r0 + width, :], axis=0, keepdims=True)
            n = jnp.where(rank1 == float(p), n_p, n)
            r0 += width
        r2_ref[h] = rank2.astype(r2_ref.dtype)
        e2_ref[h] = jnp.exp(s2 - v2_ref[0:1, :]).astype(e2_ref.dtype)
        n_ref[h] = n
        a_ref[h] = jnp.exp(s1 - v1_ref[0:1, :]) / z
        return carry

    lax.fori_loop(0, r2_ref.shape[0], head, 0)


def _peer_topk(sT):
    groups, nk, t = sT.shape
    heads = groups // 2
    tm = min(PEER_TOPK_TM, t)
    out = pl.BlockSpec((heads, nk, tm), lambda i: (0, 0, i))
    wide = jax.ShapeDtypeStruct((heads, nk, t), F32)
    narrow = jax.ShapeDtypeStruct((heads, nk, t), BF16)
    return pl.pallas_call(
        _peer_topk_kernel,
        grid=(t // tm,),
        in_specs=[pl.BlockSpec((groups, nk, tm), lambda i: (0, 0, i))],
        out_specs=[out, out, out, out],
        out_shape=[narrow, narrow, wide, wide],
        scratch_shapes=[pltpu.VMEM((PEER_TOPK, tm), F32), pltpu.VMEM((PEER_TOPK, tm), F32),
                        pltpu.VMEM((_CAND_ROWS, tm), F32), pltpu.VMEM((_CAND_ROWS, tm), F32)],
        compiler_params=_params("parallel"),
        name="peer_topk",
    )(sT)


def _peer_dense_kernel(xT_ref, u_ref, vT_ref, r2_ref, e2_ref, n_ref, a_ref, y_ref, act_ref):
    e = pl.program_id(1)
    heads, nk, _ = r2_ref.shape
    ni = u_ref.shape[0] // nk

    @pl.when(e == 0)
    def _():
        y_ref[...] = jnp.zeros_like(y_ref)

    zero = jnp.zeros((), e2_ref.dtype)
    group = ni // PEER_GROUPS
    for gi in range(PEER_GROUPS):
        rows = slice(gi * group * nk, (gi + 1) * group * nk)
        hT = jnp.dot(u_ref[rows, :], xT_ref[...], preferred_element_type=F32)
        for ii in range(group):
            i = e * ni + gi * group + ii
            w = None
            for h in range(heads):
                n = n_ref[h, pl.ds(i, 1), :].astype(r2_ref.dtype)
                a = a_ref[h, pl.ds(i, 1), :].astype(e2_ref.dtype)
                term = jnp.where(r2_ref[h] < n, e2_ref[h], zero) * a
                w = term if w is None else w + term
            g = jax.nn.gelu(hT[ii * nk:(ii + 1) * nk]).astype(w.dtype)
            act_ref[(gi * group + ii) * nk:(gi * group + ii + 1) * nk, :] = g * w
        y_ref[...] += jnp.dot(vT_ref[:, rows], act_ref[rows, :], preferred_element_type=F32)


def _peer_dense(xT, u, vT, r2, e2, n, a):
    d, t = xT.shape
    n_exp = u.shape[0]
    heads, nk, _ = r2.shape
    tm = min(PEER_TM, t)
    te = PEER_NI * nk
    sel = pl.BlockSpec((heads, nk, tm), lambda i, e: (0, 0, i))
    return pl.pallas_call(
        _peer_dense_kernel,
        grid=(t // tm, n_exp // te),
        in_specs=[pl.BlockSpec((d, tm), lambda i, e: (0, i)),
                  pl.BlockSpec((te, d), lambda i, e: (e, 0)),
                  pl.BlockSpec((d, te), lambda i, e: (0, e)),
                  sel, sel, sel, sel],
        out_specs=pl.BlockSpec((d, tm), lambda i, e: (0, i)),
        out_shape=jax.ShapeDtypeStruct((d, t), F32),
        scratch_shapes=[pltpu.VMEM((te, tm), BF16)],
        compiler_params=_params("parallel", "arbitrary"),
        name="peer_dense",
    )(xT, u, vT, r2, e2, n, a)


def _peer(xb, wq, sub_keys, u_tab, v_tab):
    xT = xb.T
    keys = sub_keys.reshape(-1, sub_keys.shape[-2], sub_keys.shape[-1]).astype(BF16)
    sT = _peer_scores(xT, wq.T.astype(BF16), keys)
    r2, e2, n, a = _peer_topk(sT)
    return _peer_dense(xT, u_tab.astype(BF16), v_tab.T.astype(BF16), r2, e2, n, a)


def kernel(x, ret_w_in, ret_gn_g, ret_w_out, kv_w, sb_wq, sb_w_out, peer_wq, peer_sub_keys, peer_u, peer_v,
           ln_g, ln_b):
    batch, seq, d_model = x.shape
    depth = ln_g.shape[0]
    n_a = ret_w_in.shape[0]
    alpha = (2.0 * depth) ** 0.25
    xf = x.reshape(batch * seq, d_model)
    xb = xf.astype(BF16)
    kv = None
    for l in range(depth):
        if l < n_a:
            proj = _matmul(xb, ret_w_in[l].astype(BF16), BF16)
            o = _retention_core(proj, ret_gn_g[l], batch, seq, d_model)
            mix = _matmul(o, ret_w_out[l].astype(BF16), F32)
        else:
            q = _matmul(xb, sb_wq[l - n_a].astype(BF16), BF16)
            o = _stick_breaking_core(q, kv, batch, seq, d_model)
            mix = _matmul(o, sb_w_out[l - n_a].astype(BF16), F32)
        xf, xb = _ln_res(xf, mix, ln_g[l, 0], ln_b[l, 0], alpha)
        yT = _peer(xb, peer_wq[l], peer_sub_keys[l], peer_u[l], peer_v[l])
        xf, xb = _ln_res(xf, yT.T, ln_g[l, 1], ln_b[l, 1], alpha)
        if l == n_a - 1:
            kv = _matmul(xb, kv_w.astype(BF16), BF16)
    return xf.reshape(batch, seq, d_model)
```

```python
import functools

import jax
import jax.numpy as jnp
from jax import lax
from jax.experimental import pallas as pl
from jax.experimental.pallas import tpu as pltpu

F32 = jnp.float32
BF16 = jnp.bfloat16

LN_EPS = 1e-5
ROPE_BASE = 10000.0
CHUNK = 64
RET_HEADS = 8
SB_HEADS = 16
PEER_HEADS = 8
PEER_TOPK = 16
N_KEYS = 128

MIB = 1024 * 1024
VMEM_LIMIT_BYTES = 56 * MIB

MM_TM, MM_TN = 1024, 1024
MM_LN_TM, MM_LN_TK = 512, 1024
LN_TM = 256
RET_BLOCK = 256
SB_TQ = SB_TK = 256
SB_EXP_ZERO = -104.0
PEER_SCORE_TM = 512
PEER_TOPK_TM = 128
PEER_TM = 512
PEER_NI = 8
PEER_GROUPS = 2

_CAND = [(p, q) for p in range(PEER_TOPK) for q in range(PEER_TOPK) if (p + 1) * (q + 1) <= PEER_TOPK]
_CAND_ROWS = -(-len(_CAND) // 8) * 8


def _params(*sem):
    return pltpu.CompilerParams(dimension_semantics=sem, vmem_limit_bytes=VMEM_LIMIT_BYTES)


def _layer_norm(z, g, b):
    mu = jnp.mean(z, axis=-1, keepdims=True)
    d = z - mu
    var = jnp.mean(d * d, axis=-1, keepdims=True)
    return d * lax.rsqrt(var + LN_EPS) * g + b


def _mm_kernel(a_ref, b_ref, o_ref):
    o_ref[...] = jnp.dot(a_ref[...], b_ref[...], preferred_element_type=F32).astype(o_ref.dtype)


def _matmul(a, b, out_dtype):
    m, k = a.shape
    n = b.shape[1]
    tm, tn = min(MM_TM, m), min(MM_TN, n)
    return pl.pallas_call(
        _mm_kernel,
        grid=(m // tm, n // tn),
        in_specs=[pl.BlockSpec((tm, k), lambda i, j: (i, 0)),
                  pl.BlockSpec((k, tn), lambda i, j: (0, j))],
        out_specs=pl.BlockSpec((tm, tn), lambda i, j: (i, j)),
        out_shape=jax.ShapeDtypeStruct((m, n), out_dtype),
        compiler_params=_params("parallel", "parallel"),
        name="mm",
    )(a, b)


def _mm_ln_kernel(a_ref, b_ref, x_ref, g_ref, beta_ref, of_ref, ob_ref, acc_ref, *, alpha):
    k = pl.program_id(1)

    @pl.when(k == 0)
    def _():
        acc_ref[...] = jnp.zeros_like(acc_ref)

    acc_ref[...] += jnp.dot(a_ref[...], b_ref[...], preferred_element_type=F32)

    @pl.when(k == pl.num_programs(1) - 1)
    def _():
        o = _layer_norm(alpha * x_ref[...] + acc_ref[...], g_ref[...], beta_ref[...])
        of_ref[...] = o
        ob_ref[...] = o.astype(ob_ref.dtype)


def _matmul_ln(a, b, x, g, beta, alpha):
    m, k = a.shape
    d = b.shape[1]
    tm, tk = min(MM_LN_TM, m), min(MM_LN_TK, k)
    row = pl.BlockSpec((tm, d), lambda i, j: (i, 0))
    vec = pl.BlockSpec((1, d), lambda i, j: (0, 0))
    return pl.pallas_call(
        functools.partial(_mm_ln_kernel, alpha=alpha),
        grid=(m // tm, k // tk),
        in_specs=[pl.BlockSpec((tm, tk), lambda i, j: (i, j)),
                  pl.BlockSpec((tk, d), lambda i, j: (j, 0)),
                  row, vec, vec],
        out_specs=[row, row],
        out_shape=[jax.ShapeDtypeStruct((m, d), F32), jax.ShapeDtypeStruct((m, d), BF16)],
        scratch_shapes=[pltpu.VMEM((tm, d), F32)],
        compiler_params=_params("parallel", "arbitrary"),
        name="mm_ln",
    )(a, b, x, g.reshape(1, d), beta.reshape(1, d))


def _ln_t_kernel(x_ref, yT_ref, g_ref, b_ref, of_ref, ob_ref, *, alpha):
    o = _layer_norm(alpha * x_ref[...] + yT_ref[...].T, g_ref[...], b_ref[...])
    of_ref[...] = o
    ob_ref[...] = o.astype(ob_ref.dtype)


def _ln_res_t(x, yT, g, b, alpha):
    t, d = x.shape
    tm = min(LN_TM, t)
    row = pl.BlockSpec((tm, d), lambda i: (i, 0))
    vec = pl.BlockSpec((1, d), lambda i: (0, 0))
    return pl.pallas_call(
        functools.partial(_ln_t_kernel, alpha=alpha),
        grid=(t // tm,),
        in_specs=[row, pl.BlockSpec((d, tm), lambda i: (0, i)), vec, vec],
        out_specs=[row, row],
        out_shape=[jax.ShapeDtypeStruct((t, d), F32), jax.ShapeDtypeStruct((t, d), BF16)],
        compiler_params=_params("parallel"),
        name="ln_res",
    )(x, yT, g.reshape(1, d), b.reshape(1, d))


def _rope(x, cos, sin):
    half = x.shape[1] // 2
    x1, x2 = x[:, :half], x[:, half:]
    return jnp.concatenate([x1 * cos - x2 * sin, x1 * sin + x2 * cos], axis=1)


def _ret_kernel(dec_ref, q_ref, k_ref, v_ref, g_ref, cos_ref, sin_ref, dmat_ref, qdec_ref, kdec_ref, gn_ref,
                o_ref, state_ref, *, scale):
    h = pl.program_id(1)

    @pl.when(pl.program_id(2) == 0)
    def _():
        state_ref[...] = jnp.zeros_like(state_ref)

    cos, sin = cos_ref[...], sin_ref[...]
    qr = _rope(q_ref[...].astype(F32), cos, sin) * scale
    kr = _rope(k_ref[...].astype(F32), cos, sin)
    v = v_ref[...]
    sc = lax.dot_general(qr.astype(BF16), kr.astype(BF16), (((1,), (1,)), ((), ())),
                         preferred_element_type=F32) * dmat_ref[0]
    o = jnp.dot(sc.astype(BF16), v, preferred_element_type=F32)
    o = o + jnp.dot((qr * qdec_ref[0]).astype(BF16), state_ref[...].astype(BF16), preferred_element_type=F32)
    kd = (kr * kdec_ref[0]).astype(BF16)
    state_ref[...] = state_ref[...] * dec_ref[h] + lax.dot_general(
        kd, v, (((0,), (0,)), ((), ())), preferred_element_type=F32)
    mu = jnp.mean(o, axis=-1, keepdims=True)
    d = o - mu
    var = jnp.mean(d * d, axis=-1, keepdims=True)
    on = d * lax.rsqrt(var + LN_EPS) * gn_ref[...]
    g = g_ref[...].astype(F32)
    o_ref[...] = (g * jax.nn.sigmoid(g) * on).astype(BF16)


def _retention_tables(seq, dk, blk):
    pos = jnp.arange(seq, dtype=F32)
    inv_freq = ROPE_BASE ** (-jnp.arange(0, dk, 2, dtype=F32) / dk)
    ang = pos[:, None] * inv_freq[None, :]
    log_gamma = jnp.log(1.0 - jnp.exp2(-5.0 - jnp.arange(RET_HEADS, dtype=F32)))
    n = jnp.arange(blk, dtype=F32)
    dist = jnp.abs(n[:, None] - n[None, :])
    chunk = jnp.arange(blk) // CHUNK
    visible = (chunk[None, :] <= chunk[:, None]).astype(F32)
    dmat = jnp.exp(log_gamma[:, None, None] * dist) * visible
    qdec = jnp.broadcast_to(jnp.exp(log_gamma[:, None] * (n + 1.0))[:, :, None], (RET_HEADS, blk, dk))
    kdec = jnp.broadcast_to(jnp.exp(log_gamma[:, None] * (blk - 1.0 - n))[:, :, None], (RET_HEADS, blk, dk))
    dec = jnp.exp(log_gamma * blk)
    return jnp.cos(ang), jnp.sin(ang), dmat, qdec, kdec, dec


def _retention_core(proj, gn_g, batch, seq, d_model):
    dk = d_model // RET_HEADS
    dv = 2 * d_model // RET_HEADS
    blk = min(RET_BLOCK, seq)
    nblk = seq // blk
    cos, sin, dmat, qdec, kdec, dec = _retention_tables(seq, dk, blk)
    k_col0 = RET_HEADS
    v_col0 = 2 * RET_HEADS * dk // dv
    g_col0 = v_col0 + RET_HEADS
    head_tab = lambda b, h, s, dec: (h, 0, 0)
    grid_spec = pltpu.PrefetchScalarGridSpec(
        num_scalar_prefetch=1,
        grid=(batch, RET_HEADS, nblk),
        in_specs=[
            pl.BlockSpec((blk, dk), lambda b, h, s, dec: (b * nblk + s, h)),
            pl.BlockSpec((blk, dk), lambda b, h, s, dec: (b * nblk + s, k_col0 + h)),
            pl.BlockSpec((blk, dv), lambda b, h, s, dec: (b * nblk + s, v_col0 + h)),
            pl.BlockSpec((blk, dv), lambda b, h, s, dec: (b * nblk + s, g_col0 + h)),
            pl.BlockSpec((blk, dk // 2), lambda b, h, s, dec: (s, 0)),
            pl.BlockSpec((blk, dk // 2), lambda b, h, s, dec: (s, 0)),
            pl.BlockSpec((1, blk, blk), head_tab),
            pl.BlockSpec((1, blk, dk), head_tab),
            pl.BlockSpec((1, blk, dk), head_tab),
            pl.BlockSpec((1, dv), lambda b, h, s, dec: (0, h)),
        ],
        out_specs=pl.BlockSpec((blk, dv), lambda b, h, s, dec: (b * nblk + s, h)),
        scratch_shapes=[pltpu.VMEM((dk, dv), F32)],
    )
    return pl.pallas_call(
        functools.partial(_ret_kernel, scale=dk ** -0.5),
        grid_spec=grid_spec,
        out_shape=jax.ShapeDtypeStruct((batch * seq, RET_HEADS * dv), BF16),
        compiler_params=_params("parallel", "parallel", "arbitrary"),
        name="retention",
    )(dec, proj, proj, proj, proj, cos, sin, dmat, qdec, kdec, gn_g.reshape(1, -1))


def _sb_kernel(q_ref, k_ref, v_ref, o_ref, *, scale):
    tq, dh = q_ref.shape
    tk = min(SB_TK, k_ref.shape[0])
    qi = pl.program_id(2)
    q = q_ref[...]
    t_idx = qi * tq + lax.broadcasted_iota(jnp.int32, (tq, tk), 0)
    s_loc = lax.broadcasted_iota(jnp.int32, (tq, tk), 1)
    later = (lax.broadcasted_iota(jnp.int32, (tk, tk), 0) > lax.broadcasted_iota(jnp.int32, (tk, tk), 1)).astype(BF16)
    n_kb = (qi + 1) * tq // tk

    def body(carry):
        it, _, c, acc = carry
        kb = n_kb - 1 - it
        off = pl.multiple_of(kb * tk, tk)
        kblk = k_ref[pl.ds(off, tk), :]
        vblk = v_ref[pl.ds(off, tk), :]
        z = lax.dot_general(q, kblk, (((1,), (1,)), ((), ())), preferred_element_type=F32) * scale
        causal = (kb * tk + s_loc) < t_idx
        log_beta = jnp.minimum(z, 0.0) - jnp.log1p(jnp.exp(-jnp.abs(z)))
        log_1m = jnp.where(causal, log_beta - z, 0.0)
        hi = log_1m.astype(BF16)
        lo = (log_1m - hi.astype(F32)).astype(BF16)
        tail = (jnp.dot(hi, later, preferred_element_type=F32)
                + jnp.dot(lo, later, preferred_element_type=F32) + c)
        w = jnp.where(causal, jnp.exp(log_beta + tail), 0.0)
        acc = acc + jnp.dot(w.astype(BF16), vblk, preferred_element_type=F32)
        c = c + jnp.sum(log_1m, axis=1, keepdims=True)
        return it + 1, jnp.max(c), c, acc

    def more(carry):
        it, c_max, _, _ = carry
        return jnp.logical_and(it < n_kb, c_max > SB_EXP_ZERO)

    init = (jnp.int32(0), jnp.float32(0.0), jnp.zeros((tq, 1), F32), jnp.zeros((tq, dh), F32))
    _, _, _, acc = lax.while_loop(more, body, init)
    o_ref[...] = acc.astype(o_ref.dtype)


def _stick_breaking_core(q, kv, batch, seq, d_model):
    dh = d_model // SB_HEADS
    tq = min(SB_TQ, seq)
    nq = seq // tq
    return pl.pallas_call(
        functools.partial(_sb_kernel, scale=dh ** -0.5),
        grid=(batch, SB_HEADS, nq),
        in_specs=[pl.BlockSpec((tq, dh), lambda b, h, i: (b * nq + i, h)),
                  pl.BlockSpec((seq, dh), lambda b, h, i: (b, h)),
                  pl.BlockSpec((seq, dh), lambda b, h, i: (b, SB_HEADS + h))],
        out_specs=pl.BlockSpec((tq, dh), lambda b, h, i: (b * nq + i, h)),
        out_shape=jax.ShapeDtypeStruct((batch * seq, d_model), BF16),
        compiler_params=_params("parallel", "parallel", "parallel"),
        name="stick_breaking",
    )(q, kv, kv)


def _peer_score_kernel(xT_ref, wqT_ref, keys_ref, sT_ref):
    qT = jnp.dot(wqT_ref[...], xT_ref[...], preferred_element_type=F32)
    dq = keys_ref.shape[2]
    for g in range(keys_ref.shape[0]):
        sT_ref[g] = jnp.dot(keys_ref[g], qT[g * dq:(g + 1) * dq].astype(BF16), preferred_element_type=F32)


def _peer_scores(xT, wqT, keys):
    d, t = xT.shape
    groups, nk, dq = keys.shape
    tm = min(PEER_SCORE_TM, t)
    return pl.pallas_call(
        _peer_score_kernel,
        grid=(t // tm,),
        in_specs=[pl.BlockSpec((d, tm), lambda i: (0, i)),
                  pl.BlockSpec(wqT.shape, lambda i: (0, 0)),
                  pl.BlockSpec(keys.shape, lambda i: (0, 0, 0))],
        out_specs=pl.BlockSpec((groups, nk, tm), lambda i: (0, 0, i)),
        out_shape=jax.ShapeDtypeStruct((groups, nk, t), F32),
        compiler_params=_params("parallel"),
        name="peer_scores",
    )(xT, wqT, keys)


def _row_ids(shape):
    return lax.broadcasted_iota(jnp.int32, shape, 0).astype(F32)


def _argmax_mask(s, row, exact):
    m = jnp.max(s, axis=0, keepdims=True)
    if not exact:
        return m, s == m
    idx = jnp.min(jnp.where(s == m, row, float(s.shape[0])), axis=0, keepdims=True)
    return m, row == idx


def _topk_rank(s, vals_ref, exact):
    row = _row_ids(s.shape)
    rank = jnp.full(s.shape, float(PEER_TOPK), F32)
    for k in range(PEER_TOPK):
        m, sel = _argmax_mask(s, row, exact)
        rank = jnp.where(sel, float(k), rank)
        s = jnp.where(sel, -jnp.inf, s)
        vals_ref[k:k + 1, :] = m
    return rank


def _peer_topk_kernel(sT_ref, r2_ref, e2_ref, n_ref, a_ref, v1_ref, v2_ref, c_ref, sel_ref):
    tm = sT_ref.shape[2]
    crow = _row_ids((_CAND_ROWS, tm))

    def rank_head(h, exact):
        s1 = sT_ref[2 * h]
        s2 = sT_ref[2 * h + 1]
        rank1 = _topk_rank(s1, v1_ref, exact)
        rank2 = _topk_rank(s2, v2_ref, exact)
        c_ref[...] = jnp.full(c_ref.shape, -jnp.inf, F32)
        for r, (p, q) in enumerate(_CAND):
            c_ref[r:r + 1, :] = v1_ref[p:p + 1, :] + v2_ref[q:q + 1, :]
        c = c_ref[...]
        chosen = jnp.zeros(c.shape, F32)
        m0 = v1_ref[0:1, :] + v2_ref[0:1, :]
        z = jnp.zeros((1, tm), F32)
        for _ in range(PEER_TOPK):
            m, sel = _argmax_mask(c, crow, exact)
            chosen = jnp.where(sel, 1.0, chosen)
            c = jnp.where(sel, -jnp.inf, c)
            z = z + jnp.exp(m - m0)
        sel_ref[...] = chosen
        n = jnp.zeros(s1.shape, F32)
        r0 = 0
        for p in range(PEER_TOPK):
            width = sum(1 for (pp, _) in _CAND if pp == p)
            n_p = jnp.sum(sel_ref[r0:r0 + width, :], axis=0, keepdims=True)
            n = jnp.where(rank1 == float(p), n_p, n)
            r0 += width
        r2_ref[h] = rank2.astype(r2_ref.dtype)
        e2_ref[h] = jnp.exp(s2 - v2_ref[0:1, :]).astype(e2_ref.dtype)
        n_ref[h] = n
        a_ref[h] = jnp.exp(s1 - v1_ref[0:1, :]) / z
        taken = jnp.maximum(
            jnp.maximum(jnp.sum(jnp.where(rank1 < PEER_TOPK, 1.0, 0.0), axis=0, keepdims=True),
                        jnp.sum(jnp.where(rank2 < PEER_TOPK, 1.0, 0.0), axis=0, keepdims=True)),
            jnp.sum(chosen, axis=0, keepdims=True))
        return jnp.max(taken)

    def head(h, carry):
        most_taken = rank_head(h, exact=False)

        @pl.when(most_taken > PEER_TOPK)
        def _():
            rank_head(h, exact=True)

        return carry

    lax.fori_loop(0, r2_ref.shape[0], head, 0)


def _peer_topk(sT):
    groups, nk, t = sT.shape
    heads = groups // 2
    tm = min(PEER_TOPK_TM, t)
    out = pl.BlockSpec((heads, nk, tm), lambda i: (0, 0, i))
    wide = jax.ShapeDtypeStruct((heads, nk, t), F32)
    narrow = jax.ShapeDtypeStruct((heads, nk, t), BF16)
    return pl.pallas_call(
        _peer_topk_kernel,
        grid=(t // tm,),
        in_specs=[pl.BlockSpec((groups, nk, tm), lambda i: (0, 0, i))],
        out_specs=[out, out, out, out],
        out_shape=[narrow, narrow, wide, wide],
        scratch_shapes=[pltpu.VMEM((PEER_TOPK, tm), F32), pltpu.VMEM((PEER_TOPK, tm), F32),
                        pltpu.VMEM((_CAND_ROWS, tm), F32), pltpu.VMEM((_CAND_ROWS, tm), F32)],
        compiler_params=_params("parallel"),
        name="peer_topk",
    )(sT)


def _peer_dense_kernel(xT_ref, u_ref, vT_ref, r2_ref, e2_ref, n_ref, a_ref, y_ref, act_ref):
    e = pl.program_id(1)
    heads, nk, _ = r2_ref.shape
    ni = u_ref.shape[0] // nk

    @pl.when(e == 0)
    def _():
        y_ref[...] = jnp.zeros_like(y_ref)

    zero = jnp.zeros((), e2_ref.dtype)
    group = ni // PEER_GROUPS
    for gi in range(PEER_GROUPS):
        rows = slice(gi * group * nk, (gi + 1) * group * nk)
        hT = jnp.dot(u_ref[rows, :], xT_ref[...], preferred_element_type=F32)
        for ii in range(group):
            i = e * ni + gi * group + ii
            w = None
            for h in range(heads):
                n = n_ref[h, pl.ds(i, 1), :].astype(r2_ref.dtype)
                a = a_ref[h, pl.ds(i, 1), :].astype(e2_ref.dtype)
                term = jnp.where(r2_ref[h] < n, e2_ref[h], zero) * a
                w = term if w is None else w + term
            g = jax.nn.gelu(hT[ii * nk:(ii + 1) * nk]).astype(w.dtype)
            act_ref[(gi * group + ii) * nk:(gi * group + ii + 1) * nk, :] = g * w
        y_ref[...] += jnp.dot(vT_ref[:, rows], act_ref[rows, :], preferred_element_type=F32)


def _peer_dense(xT, u, vT, r2, e2, n, a):
    d, t = xT.shape
    n_exp = u.shape[0]
    heads, nk, _ = r2.shape
    tm = min(PEER_TM, t)
    te = PEER_NI * nk
    sel = pl.BlockSpec((heads, nk, tm), lambda i, e: (0, 0, i))
    return pl.pallas_call(
        _peer_dense_kernel,
        grid=(t // tm, n_exp // te),
        in_specs=[pl.BlockSpec((d, tm), lambda i, e: (0, i)),
                  pl.BlockSpec((te, d), lambda i, e: (e, 0)),
                  pl.BlockSpec((d, te), lambda i, e: (0, e)),
                  sel, sel, sel, sel],
        out_specs=pl.BlockSpec((d, tm), lambda i, e: (0, i)),
        out_shape=jax.ShapeDtypeStruct((d, t), F32),
        scratch_shapes=[pltpu.VMEM((te, tm), BF16)],
        compiler_params=_params("parallel", "arbitrary"),
        name="peer_dense",
    )(xT, u, vT, r2, e2, n, a)


def _peer(xb, wq, sub_keys, u_tab, v_tab):
    xT = xb.T
    keys = sub_keys.reshape(-1, sub_keys.shape[-2], sub_keys.shape[-1]).astype(BF16)
    sT = _peer_scores(xT, wq.T.astype(BF16), keys)
    r2, e2, n, a = _peer_topk(sT)
    return _peer_dense(xT, u_tab.astype(BF16), v_tab.T.astype(BF16), r2, e2, n, a)


def kernel(x, ret_w_in, ret_gn_g, ret_w_out, kv_w, sb_wq, sb_w_out, peer_wq, peer_sub_keys, peer_u, peer_v,
           ln_g, ln_b):
    batch, seq, d_model = x.shape
    depth = ln_g.shape[0]
    n_a = ret_w_in.shape[0]
    alpha = (2.0 * depth) ** 0.25
    xf = x.reshape(batch * seq, d_model)
    xb = xf.astype(BF16)
    kv = None
    for l in range(depth):
        if l < n_a:
            proj = _matmul(xb, ret_w_in[l].astype(BF16), BF16)
            o = _retention_core(proj, ret_gn_g[l], batch, seq, d_model)
            w_out = ret_w_out[l]
        else:
            q = _matmul(xb, sb_wq[l - n_a].astype(BF16), BF16)
            o = _stick_breaking_core(q, kv, batch, seq, d_model)
            w_out = sb_w_out[l - n_a]
        xf, xb = _matmul_ln(o, w_out.astype(BF16), xf, ln_g[l, 0], ln_b[l, 0], alpha)
        yT = _peer(xb, peer_wq[l], peer_sub_keys[l], peer_u[l], peer_v[l])
        xf, xb = _ln_res_t(xf, yT, ln_g[l, 1], ln_b[l, 1], alpha)
        if l == n_a - 1:
            kv = _matmul(xb, kv_w.astype(BF16), BF16)
    return xf.reshape(batch, seq, d_model)
```

```python
import functools

import jax
import jax.numpy as jnp
from jax import lax
from jax.experimental import pallas as pl
from jax.experimental.pallas import tpu as pltpu

F32 = jnp.float32
BF16 = jnp.bfloat16

LN_EPS = 1e-5
ROPE_BASE = 10000.0
CHUNK = 64
RET_HEADS = 8
SB_HEADS = 16
PEER_HEADS = 8
PEER_TOPK = 16
N_KEYS = 128

MIB = 1024 * 1024
VMEM_LIMIT_BYTES = 56 * MIB

MM_TM, MM_TN = 1024, 1024
MM_LN_TM, MM_LN_TK = 512, 1024
LN_TM = 256
RET_BLOCK = 256
SB_TQ = SB_TK = 256
SB_HEADS_PER_STEP = 2
SB_EXP_ZERO = -104.0
PEER_SCORE_TM = 512
PEER_TOPK_TM = 128
PEER_TOPK_HEADS_PER_STEP = 2
PEER_TM = 512
PEER_NI = 8
PEER_GROUPS = 2

_CAND = [(p, q) for p in range(PEER_TOPK) for q in range(PEER_TOPK) if (p + 1) * (q + 1) <= PEER_TOPK]
_CAND_ROWS = -(-len(_CAND) // 8) * 8


def _params(*sem):
    return pltpu.CompilerParams(dimension_semantics=sem, vmem_limit_bytes=VMEM_LIMIT_BYTES)


def _layer_norm(z, g, b):
    mu = jnp.mean(z, axis=-1, keepdims=True)
    d = z - mu
    var = jnp.mean(d * d, axis=-1, keepdims=True)
    return d * lax.rsqrt(var + LN_EPS) * g + b


def _mm_kernel(a_ref, b_ref, o_ref):
    o_ref[...] = jnp.dot(a_ref[...], b_ref[...], preferred_element_type=F32).astype(o_ref.dtype)


def _matmul(a, b, out_dtype):
    m, k = a.shape
    n = b.shape[1]
    tm, tn = min(MM_TM, m), min(MM_TN, n)
    return pl.pallas_call(
        _mm_kernel,
        grid=(m // tm, n // tn),
        in_specs=[pl.BlockSpec((tm, k), lambda i, j: (i, 0)),
                  pl.BlockSpec((k, tn), lambda i, j: (0, j))],
        out_specs=pl.BlockSpec((tm, tn), lambda i, j: (i, j)),
        out_shape=jax.ShapeDtypeStruct((m, n), out_dtype),
        compiler_params=_params("parallel", "parallel"),
        name="mm",
    )(a, b)


def _mm_ln_kernel(a_ref, b_ref, x_ref, g_ref, beta_ref, of_ref, ob_ref, acc_ref, *, alpha):
    k = pl.program_id(1)

    @pl.when(k == 0)
    def _():
        acc_ref[...] = jnp.zeros_like(acc_ref)

    acc_ref[...] += jnp.dot(a_ref[...], b_ref[...], preferred_element_type=F32)

    @pl.when(k == pl.num_programs(1) - 1)
    def _():
        o = _layer_norm(alpha * x_ref[...] + acc_ref[...], g_ref[...], beta_ref[...])
        of_ref[...] = o
        ob_ref[...] = o.astype(ob_ref.dtype)


def _matmul_ln(a, b, x, g, beta, alpha):
    m, k = a.shape
    d = b.shape[1]
    tm, tk = min(MM_LN_TM, m), min(MM_LN_TK, k)
    row = pl.BlockSpec((tm, d), lambda i, j: (i, 0))
    vec = pl.BlockSpec((1, d), lambda i, j: (0, 0))
    return pl.pallas_call(
        functools.partial(_mm_ln_kernel, alpha=alpha),
        grid=(m // tm, k // tk),
        in_specs=[pl.BlockSpec((tm, tk), lambda i, j: (i, j)),
                  pl.BlockSpec((tk, d), lambda i, j: (j, 0)),
                  row, vec, vec],
        out_specs=[row, row],
        out_shape=[jax.ShapeDtypeStruct((m, d), F32), jax.ShapeDtypeStruct((m, d), BF16)],
        scratch_shapes=[pltpu.VMEM((tm, d), F32)],
        compiler_params=_params("parallel", "arbitrary"),
        name="mm_ln",
    )(a, b, x, g.reshape(1, d), beta.reshape(1, d))


def _ln_t_kernel(x_ref, yT_ref, g_ref, b_ref, of_ref, ob_ref, *, alpha):
    o = _layer_norm(alpha * x_ref[...] + yT_ref[...].T, g_ref[...], b_ref[...])
    of_ref[...] = o
    ob_ref[...] = o.astype(ob_ref.dtype)


def _ln_res_t(x, yT, g, b, alpha):
    t, d = x.shape
    tm = min(LN_TM, t)
    row = pl.BlockSpec((tm, d), lambda i: (i, 0))
    vec = pl.BlockSpec((1, d), lambda i: (0, 0))
    return pl.pallas_call(
        functools.partial(_ln_t_kernel, alpha=alpha),
        grid=(t // tm,),
        in_specs=[row, pl.BlockSpec((d, tm), lambda i: (0, i)), vec, vec],
        out_specs=[row, row],
        out_shape=[jax.ShapeDtypeStruct((t, d), F32), jax.ShapeDtypeStruct((t, d), BF16)],
        compiler_params=_params("parallel"),
        name="ln_res",
    )(x, yT, g.reshape(1, d), b.reshape(1, d))


def _rope(x, cos, sin):
    half = x.shape[1] // 2
    x1, x2 = x[:, :half], x[:, half:]
    return jnp.concatenate([x1 * cos - x2 * sin, x1 * sin + x2 * cos], axis=1)


def _ret_kernel(dec_ref, q_ref, k_ref, v_ref, g_ref, cos_ref, sin_ref, dmat_ref, qdec_ref, kdec_ref, gn_ref,
                o_ref, state_ref, *, scale):
    h = pl.program_id(1)

    @pl.when(pl.program_id(2) == 0)
    def _():
        state_ref[...] = jnp.zeros_like(state_ref)

    cos, sin = cos_ref[...], sin_ref[...]
    qr = _rope(q_ref[...].astype(F32), cos, sin) * scale
    kr = _rope(k_ref[...].astype(F32), cos, sin)
    v = v_ref[...]
    sc = lax.dot_general(qr.astype(BF16), kr.astype(BF16), (((1,), (1,)), ((), ())),
                         preferred_element_type=F32) * dmat_ref[0]
    o = jnp.dot(sc.astype(BF16), v, preferred_element_type=F32)
    o = o + jnp.dot((qr * qdec_ref[0]).astype(BF16), state_ref[...].astype(BF16), preferred_element_type=F32)
    kd = (kr * kdec_ref[0]).astype(BF16)
    state_ref[...] = state_ref[...] * dec_ref[h] + lax.dot_general(
        kd, v, (((0,), (0,)), ((), ())), preferred_element_type=F32)
    mu = jnp.mean(o, axis=-1, keepdims=True)
    d = o - mu
    var = jnp.mean(d * d, axis=-1, keepdims=True)
    on = d * lax.rsqrt(var + LN_EPS) * gn_ref[...]
    g = g_ref[...].astype(F32)
    o_ref[...] = (g * jax.nn.sigmoid(g) * on).astype(BF16)


def _retention_tables(seq, dk, blk):
    pos = jnp.arange(seq, dtype=F32)
    inv_freq = ROPE_BASE ** (-jnp.arange(0, dk, 2, dtype=F32) / dk)
    ang = pos[:, None] * inv_freq[None, :]
    log_gamma = jnp.log(1.0 - jnp.exp2(-5.0 - jnp.arange(RET_HEADS, dtype=F32)))
    n = jnp.arange(blk, dtype=F32)
    dist = jnp.abs(n[:, None] - n[None, :])
    chunk = jnp.arange(blk) // CHUNK
    visible = (chunk[None, :] <= chunk[:, None]).astype(F32)
    dmat = jnp.exp(log_gamma[:, None, None] * dist) * visible
    qdec = jnp.broadcast_to(jnp.exp(log_gamma[:, None] * (n + 1.0))[:, :, None], (RET_HEADS, blk, dk))
    kdec = jnp.broadcast_to(jnp.exp(log_gamma[:, None] * (blk - 1.0 - n))[:, :, None], (RET_HEADS, blk, dk))
    dec = jnp.exp(log_gamma * blk)
    return jnp.cos(ang), jnp.sin(ang), dmat, qdec, kdec, dec


def _retention_core(proj, gn_g, batch, seq, d_model):
    dk = d_model // RET_HEADS
    dv = 2 * d_model // RET_HEADS
    blk = min(RET_BLOCK, seq)
    nblk = seq // blk
    cos, sin, dmat, qdec, kdec, dec = _retention_tables(seq, dk, blk)
    k_col0 = RET_HEADS
    v_col0 = 2 * RET_HEADS * dk // dv
    g_col0 = v_col0 + RET_HEADS
    head_tab = lambda b, h, s, dec: (h, 0, 0)
    grid_spec = pltpu.PrefetchScalarGridSpec(
        num_scalar_prefetch=1,
        grid=(batch, RET_HEADS, nblk),
        in_specs=[
            pl.BlockSpec((blk, dk), lambda b, h, s, dec: (b * nblk + s, h)),
            pl.BlockSpec((blk, dk), lambda b, h, s, dec: (b * nblk + s, k_col0 + h)),
            pl.BlockSpec((blk, dv), lambda b, h, s, dec: (b * nblk + s, v_col0 + h)),
            pl.BlockSpec((blk, dv), lambda b, h, s, dec: (b * nblk + s, g_col0 + h)),
            pl.BlockSpec((blk, dk // 2), lambda b, h, s, dec: (s, 0)),
            pl.BlockSpec((blk, dk // 2), lambda b, h, s, dec: (s, 0)),
            pl.BlockSpec((1, blk, blk), head_tab),
            pl.BlockSpec((1, blk, dk), head_tab),
            pl.BlockSpec((1, blk, dk), head_tab),
            pl.BlockSpec((1, dv), lambda b, h, s, dec: (0, h)),
        ],
        out_specs=pl.BlockSpec((blk, dv), lambda b, h, s, dec: (b * nblk + s, h)),
        scratch_shapes=[pltpu.VMEM((dk, dv), F32)],
    )
    return pl.pallas_call(
        functools.partial(_ret_kernel, scale=dk ** -0.5),
        grid_spec=grid_spec,
        out_shape=jax.ShapeDtypeStruct((batch * seq, RET_HEADS * dv), BF16),
        compiler_params=_params("parallel", "parallel", "arbitrary"),
        name="retention",
    )(dec, proj, proj, proj, proj, cos, sin, dmat, qdec, kdec, gn_g.reshape(1, -1))


def _sb_kernel(q_ref, k_ref, v_ref, o_ref, *, scale, dh):
    tq = q_ref.shape[0]
    n_heads = q_ref.shape[1] // dh
    tk = min(SB_TK, k_ref.shape[0])
    qi = pl.program_id(2)
    t_idx = qi * tq + lax.broadcasted_iota(jnp.int32, (tq, tk), 0)
    s_loc = lax.broadcasted_iota(jnp.int32, (tq, tk), 1)
    later = (lax.broadcasted_iota(jnp.int32, (tk, tk), 0) > lax.broadcasted_iota(jnp.int32, (tk, tk), 1)).astype(BF16)
    n_kb = (qi + 1) * tq // tk

    def one_head(q, kblk, vblk, causal, c, acc):
        z = lax.dot_general(q, kblk, (((1,), (1,)), ((), ())), preferred_element_type=F32) * scale
        log_beta = jnp.minimum(z, 0.0) - jnp.log(1.0 + jnp.exp(-jnp.abs(z)))
        log_1m = jnp.where(causal, log_beta - z, 0.0)
        hi = log_1m.astype(BF16)
        lo = (log_1m - hi.astype(F32)).astype(BF16)
        tail = (jnp.dot(hi, later, preferred_element_type=F32)
                + jnp.dot(lo, later, preferred_element_type=F32) + c)
        w = jnp.where(causal, jnp.exp(log_beta + tail), 0.0)
        acc = acc + jnp.dot(w.astype(BF16), vblk, preferred_element_type=F32)
        return c + jnp.sum(log_1m, axis=1, keepdims=True), acc

    def body(carry):
        it, _, state = carry
        kb = n_kb - 1 - it
        off = pl.multiple_of(kb * tk, tk)
        causal = (kb * tk + s_loc) < t_idx
        new_state = []
        c_max = None
        for u, (c, acc) in enumerate(state):
            lanes = slice(u * dh, (u + 1) * dh)
            c, acc = one_head(q_ref[:, lanes], k_ref[pl.ds(off, tk), lanes], v_ref[pl.ds(off, tk), lanes],
                              causal, c, acc)
            new_state.append((c, acc))
            c_max = jnp.max(c) if c_max is None else jnp.maximum(c_max, jnp.max(c))
        return it + 1, c_max, tuple(new_state)

    def more(carry):
        it, c_max, _ = carry
        return jnp.logical_and(it < n_kb, c_max > SB_EXP_ZERO)

    state = tuple((jnp.zeros((tq, 1), F32), jnp.zeros((tq, dh), F32)) for _ in range(n_heads))
    _, _, state = lax.while_loop(more, body, (jnp.int32(0), jnp.float32(0.0), state))
    for u, (_, acc) in enumerate(state):
        o_ref[:, u * dh:(u + 1) * dh] = acc.astype(o_ref.dtype)


def _stick_breaking_core(q, kv, batch, seq, d_model):
    dh = d_model // SB_HEADS
    tq = min(SB_TQ, seq)
    nq = seq // tq
    wide = SB_HEADS_PER_STEP * dh
    n_groups = SB_HEADS // SB_HEADS_PER_STEP
    return pl.pallas_call(
        functools.partial(_sb_kernel, scale=dh ** -0.5, dh=dh),
        grid=(batch, n_groups, nq),
        in_specs=[pl.BlockSpec((tq, wide), lambda b, h, i: (b * nq + i, h)),
                  pl.BlockSpec((seq, wide), lambda b, h, i: (b, h)),
                  pl.BlockSpec((seq, wide), lambda b, h, i: (b, n_groups + h))],
        out_specs=pl.BlockSpec((tq, wide), lambda b, h, i: (b * nq + i, h)),
        out_shape=jax.ShapeDtypeStruct((batch * seq, d_model), BF16),
        compiler_params=_params("parallel", "parallel", "parallel"),
        name="stick_breaking",
    )(q, kv, kv)


def _peer_score_kernel(xT_ref, wqT_ref, keys_ref, sT_ref):
    qT = jnp.dot(wqT_ref[...], xT_ref[...], preferred_element_type=F32)
    dq = keys_ref.shape[2]
    for g in range(keys_ref.shape[0]):
        sT_ref[g] = jnp.dot(keys_ref[g], qT[g * dq:(g + 1) * dq].astype(BF16), preferred_element_type=F32)


def _peer_scores(xT, wqT, keys):
    d, t = xT.shape
    groups, nk, dq = keys.shape
    tm = min(PEER_SCORE_TM, t)
    return pl.pallas_call(
        _peer_score_kernel,
        grid=(t // tm,),
        in_specs=[pl.BlockSpec((d, tm), lambda i: (0, i)),
                  pl.BlockSpec(wqT.shape, lambda i: (0, 0)),
                  pl.BlockSpec(keys.shape, lambda i: (0, 0, 0))],
        out_specs=pl.BlockSpec((groups, nk, tm), lambda i: (0, 0, i)),
        out_shape=jax.ShapeDtypeStruct((groups, nk, t), F32),
        compiler_params=_params("parallel"),
        name="peer_scores",
    )(xT, wqT, keys)


def _row_ids(shape):
    return lax.broadcasted_iota(jnp.int32, shape, 0).astype(F32)


def _argmax_mask(s, row, exact):
    m = jnp.max(s, axis=0, keepdims=True)
    if not exact:
        return m, s == m
    idx = jnp.min(jnp.where(s == m, row, float(s.shape[0])), axis=0, keepdims=True)
    return m, row == idx


def _topk_rank(s, vals_ref, exact):
    row = _row_ids(s.shape)
    rank = jnp.full(s.shape, float(PEER_TOPK), F32)
    for k in range(PEER_TOPK):
        m, sel = _argmax_mask(s, row, exact)
        rank = jnp.where(sel, float(k), rank)
        s = jnp.where(sel, -jnp.inf, s)
        vals_ref[k:k + 1, :] = m
    return rank


def _peer_topk_kernel(sT_ref, r2_ref, e2_ref, n_ref, a_ref, v1s_ref, v2s_ref, cs_ref, sels_ref):
    tm = sT_ref.shape[2]
    crow = _row_ids((_CAND_ROWS, tm))

    def rank_head(h, slot, exact):
        v1_ref, v2_ref, c_ref, sel_ref = (ref.at[slot] for ref in (v1s_ref, v2s_ref, cs_ref, sels_ref))
        s1 = sT_ref[2 * h]
        s2 = sT_ref[2 * h + 1]
        rank1 = _topk_rank(s1, v1_ref, exact)
        rank2 = _topk_rank(s2, v2_ref, exact)
        c_ref[...] = jnp.full(c_ref.shape, -jnp.inf, F32)
        for r, (p, q) in enumerate(_CAND):
            c_ref[r:r + 1, :] = v1_ref[p:p + 1, :] + v2_ref[q:q + 1, :]
        c = c_ref[...]
        chosen = jnp.zeros(c.shape, F32)
        m0 = v1_ref[0:1, :] + v2_ref[0:1, :]
        z = jnp.zeros((1, tm), F32)
        for _ in range(PEER_TOPK):
            m, sel = _argmax_mask(c, crow, exact)
            chosen = jnp.where(sel, 1.0, chosen)
            c = jnp.where(sel, -jnp.inf, c)
            z = z + jnp.exp(m - m0)
        sel_ref[...] = chosen
        n = jnp.zeros(s1.shape, F32)
        r0 = 0
        for p in range(PEER_TOPK):
            width = sum(1 for (pp, _) in _CAND if pp == p)
            n_p = jnp.sum(sel_ref[r0:r0 + width, :], axis=0, keepdims=True)
            n = jnp.where(rank1 == float(p), n_p, n)
            r0 += width
        r2_ref[h] = rank2.astype(r2_ref.dtype)
        e2_ref[h] = jnp.exp(s2 - v2_ref[0:1, :]).astype(e2_ref.dtype)
        n_ref[h] = n
        a_ref[h] = jnp.exp(s1 - v1_ref[0:1, :]) / z
        taken = jnp.maximum(
            jnp.maximum(jnp.sum(jnp.where(rank1 < PEER_TOPK, 1.0, 0.0), axis=0, keepdims=True),
                        jnp.sum(jnp.where(rank2 < PEER_TOPK, 1.0, 0.0), axis=0, keepdims=True)),
            jnp.sum(chosen, axis=0, keepdims=True))
        return jnp.max(taken)

    def head_group(hg, carry):
        heads = [hg * PEER_TOPK_HEADS_PER_STEP + u for u in range(PEER_TOPK_HEADS_PER_STEP)]
        most_taken = [rank_head(h, u, exact=False) for u, h in enumerate(heads)]
        for u, h in enumerate(heads):
            @pl.when(most_taken[u] > PEER_TOPK)
            def _():
                rank_head(h, u, exact=True)

        return carry

    lax.fori_loop(0, r2_ref.shape[0] // PEER_TOPK_HEADS_PER_STEP, head_group, 0)


def _peer_topk(sT):
    groups, nk, t = sT.shape
    heads = groups // 2
    tm = min(PEER_TOPK_TM, t)
    out = pl.BlockSpec((heads, nk, tm), lambda i: (0, 0, i))
    wide = jax.ShapeDtypeStruct((heads, nk, t), F32)
    narrow = jax.ShapeDtypeStruct((heads, nk, t), BF16)
    return pl.pallas_call(
        _peer_topk_kernel,
        grid=(t // tm,),
        in_specs=[pl.BlockSpec((groups, nk, tm), lambda i: (0, 0, i))],
        out_specs=[out, out, out, out],
        out_shape=[narrow, narrow, wide, wide],
        scratch_shapes=[pltpu.VMEM((PEER_TOPK_HEADS_PER_STEP, rows, tm), F32)
                        for rows in (PEER_TOPK, PEER_TOPK, _CAND_ROWS, _CAND_ROWS)],
        compiler_params=_params("parallel"),
        name="peer_topk",
    )(sT)


def _peer_dense_kernel(xT_ref, u_ref, vT_ref, r2_ref, e2_ref, n_ref, a_ref, y_ref, act_ref):
    e = pl.program_id(1)
    heads, nk, _ = r2_ref.shape
    ni = u_ref.shape[0] // nk

    @pl.when(e == 0)
    def _():
        y_ref[...] = jnp.zeros_like(y_ref)

    zero = jnp.zeros((), e2_ref.dtype)
    group = ni // PEER_GROUPS
    for gi in range(PEER_GROUPS):
        rows = slice(gi * group * nk, (gi + 1) * group * nk)
        hT = jnp.dot(u_ref[rows, :], xT_ref[...], preferred_element_type=F32)
        for ii in range(group):
            i = e * ni + gi * group + ii
            w = None
            for h in range(heads):
                n = n_ref[h, pl.ds(i, 1), :].astype(r2_ref.dtype)
                a = a_ref[h, pl.ds(i, 1), :].astype(e2_ref.dtype)
                term = jnp.where(r2_ref[h] < n, e2_ref[h], zero) * a
                w = term if w is None else w + term
            g = jax.nn.gelu(hT[ii * nk:(ii + 1) * nk]).astype(w.dtype)
            act_ref[(gi * group + ii) * nk:(gi * group + ii + 1) * nk, :] = g * w
        y_ref[...] += jnp.dot(vT_ref[:, rows], act_ref[rows, :], preferred_element_type=F32)


def _peer_dense(xT, u, vT, r2, e2, n, a):
    d, t = xT.shape
    n_exp = u.shape[0]
    heads, nk, _ = r2.shape
    tm = min(PEER_TM, t)
    te = PEER_NI * nk
    sel = pl.BlockSpec((heads, nk, tm), lambda i, e: (0, 0, i))
    return pl.pallas_call(
        _peer_dense_kernel,
        grid=(t // tm, n_exp // te),
        in_specs=[pl.BlockSpec((d, tm), lambda i, e: (0, i)),
                  pl.BlockSpec((te, d), lambda i, e: (e, 0)),
                  pl.BlockSpec((d, te), lambda i, e: (0, e)),
                  sel, sel, sel, sel],
        out_specs=pl.BlockSpec((d, tm), lambda i, e: (0, i)),
        out_shape=jax.ShapeDtypeStruct((d, t), F32),
        scratch_shapes=[pltpu.VMEM((te, tm), BF16)],
        compiler_params=_params("parallel", "arbitrary"),
        name="peer_dense",
    )(xT, u, vT, r2, e2, n, a)


def _peer(xb, wq, sub_keys, u_tab, v_tab):
    xT = xb.T
    keys = sub_keys.reshape(-1, sub_keys.shape[-2], sub_keys.shape[-1]).astype(BF16)
    sT = _peer_scores(xT, wq.T.astype(BF16), keys)
    r2, e2, n, a = _peer_topk(sT)
    return _peer_dense(xT, u_tab.astype(BF16), v_tab.T.astype(BF16), r2, e2, n, a)


def kernel(x, ret_w_in, ret_gn_g, ret_w_out, kv_w, sb_wq, sb_w_out, peer_wq, peer_sub_keys, peer_u, peer_v,
           ln_g, ln_b):
    batch, seq, d_model = x.shape
    depth = ln_g.shape[0]
    n_a = ret_w_in.shape[0]
    alpha = (2.0 * depth) ** 0.25
    xf = x.reshape(batch * seq, d_model)
    xb = xf.astype(BF16)
    kv = None
    for l in range(depth):
        if l < n_a:
            proj = _matmul(xb, ret_w_in[l].astype(BF16), BF16)
            o = _retention_core(proj, ret_gn_g[l], batch, seq, d_model)
            w_out = ret_w_out[l]
        else:
            q = _matmul(xb, sb_wq[l - n_a].astype(BF16), BF16)
            o = _stick_breaking_core(q, kv, batch, seq, d_model)
            w_out = sb_w_out[l - n_a]
        xf, xb = _matmul_ln(o, w_out.astype(BF16), xf, ln_g[l, 0], ln_b[l, 0], alpha)
        yT = _peer(xb, peer_wq[l], peer_sub_keys[l], peer_u[l], peer_v[l])
        xf, xb = _ln_res_t(xf, yT, ln_g[l, 1], ln_b[l, 1], alpha)
        if l == n_a - 1:
            kv = _matmul(xb, kv_w.astype(BF16), BF16)
    return xf.reshape(batch, seq, d_model)
```

```python
import functools

import jax
import jax.numpy as jnp
from jax import lax
from jax.experimental import pallas as pl
from jax.experimental.pallas import tpu as pltpu

F32 = jnp.float32
BF16 = jnp.bfloat16

LN_EPS = 1e-5
ROPE_BASE = 10000.0
CHUNK = 64
RET_HEADS = 8
SB_HEADS = 16
PEER_HEADS = 8
PEER_TOPK = 16
N_KEYS = 128

MIB = 1024 * 1024
VMEM_LIMIT_BYTES = 56 * MIB

MM_TM, MM_TN = 1024, 1024
MM_LN_TM, MM_LN_TK = 512, 2048
LN_TM = 256
RET_BLOCK = 512
SB_TQ = SB_TK = 256
SB_HEADS_PER_STEP = 2
SB_EXP_ZERO = -104.0
PEER_SCORE_TM = 512
PEER_TOPK_TM = 128
PEER_TOPK_HEADS_PER_STEP = 2
PEER_TM = 512
PEER_NI = 8
PEER_GROUPS = 2

_CAND = [(p, q) for p in range(PEER_TOPK) for q in range(PEER_TOPK) if (p + 1) * (q + 1) <= PEER_TOPK]
_CAND_ROWS = -(-len(_CAND) // 8) * 8


def _params(*sem):
    return pltpu.CompilerParams(dimension_semantics=sem, vmem_limit_bytes=VMEM_LIMIT_BYTES)


def _layer_norm(z, g, b):
    mu = jnp.mean(z, axis=-1, keepdims=True)
    d = z - mu
    var = jnp.mean(d * d, axis=-1, keepdims=True)
    return d * lax.rsqrt(var + LN_EPS) * g + b


def _mm_kernel(a_ref, b_ref, o_ref):
    o_ref[...] = jnp.dot(a_ref[...], b_ref[...], preferred_element_type=F32).astype(o_ref.dtype)


def _matmul(a, b, out_dtype):
    m, k = a.shape
    n = b.shape[1]
    tm, tn = min(MM_TM, m), min(MM_TN, n)
    return pl.pallas_call(
        _mm_kernel,
        grid=(m // tm, n // tn),
        in_specs=[pl.BlockSpec((tm, k), lambda i, j: (i, 0)),
                  pl.BlockSpec((k, tn), lambda i, j: (0, j))],
        out_specs=pl.BlockSpec((tm, tn), lambda i, j: (i, j)),
        out_shape=jax.ShapeDtypeStruct((m, n), out_dtype),
        compiler_params=_params("parallel", "parallel"),
        name="mm",
    )(a, b)


def _mm_ln_kernel(a_ref, b_ref, x_ref, g_ref, beta_ref, of_ref, ob_ref, acc_ref, *, alpha):
    k = pl.program_id(1)

    @pl.when(k == 0)
    def _():
        acc_ref[...] = jnp.zeros_like(acc_ref)

    acc_ref[...] += jnp.dot(a_ref[...], b_ref[...], preferred_element_type=F32)

    @pl.when(k == pl.num_programs(1) - 1)
    def _():
        o = _layer_norm(alpha * x_ref[...] + acc_ref[...], g_ref[...], beta_ref[...])
        of_ref[...] = o
        ob_ref[...] = o.astype(ob_ref.dtype)


def _matmul_ln(a, b, x, g, beta, alpha):
    m, k = a.shape
    d = b.shape[1]
    tm, tk = min(MM_LN_TM, m), min(MM_LN_TK, k)
    row = pl.BlockSpec((tm, d), lambda i, j: (i, 0))
    vec = pl.BlockSpec((1, d), lambda i, j: (0, 0))
    return pl.pallas_call(
        functools.partial(_mm_ln_kernel, alpha=alpha),
        grid=(m // tm, k // tk),
        in_specs=[pl.BlockSpec((tm, tk), lambda i, j: (i, j)),
                  pl.BlockSpec((tk, d), lambda i, j: (j, 0)),
                  row, vec, vec],
        out_specs=[row, row],
        out_shape=[jax.ShapeDtypeStruct((m, d), F32), jax.ShapeDtypeStruct((m, d), BF16)],
        scratch_shapes=[pltpu.VMEM((tm, d), F32)],
        compiler_params=_params("parallel", "arbitrary"),
        name="mm_ln",
    )(a, b, x, g.reshape(1, d), beta.reshape(1, d))


def _ln_t_kernel(x_ref, yT_ref, g_ref, b_ref, of_ref, ob_ref, *, alpha):
    o = _layer_norm(alpha * x_ref[...] + yT_ref[...].T, g_ref[...], b_ref[...])
    of_ref[...] = o
    ob_ref[...] = o.astype(ob_ref.dtype)


def _ln_res_t(x, yT, g, b, alpha):
    t, d = x.shape
    tm = min(LN_TM, t)
    row = pl.BlockSpec((tm, d), lambda i: (i, 0))
    vec = pl.BlockSpec((1, d), lambda i: (0, 0))
    return pl.pallas_call(
        functools.partial(_ln_t_kernel, alpha=alpha),
        grid=(t // tm,),
        in_specs=[row, pl.BlockSpec((d, tm), lambda i: (0, i)), vec, vec],
        out_specs=[row, row],
        out_shape=[jax.ShapeDtypeStruct((t, d), F32), jax.ShapeDtypeStruct((t, d), BF16)],
        compiler_params=_params("parallel"),
        name="ln_res",
    )(x, yT, g.reshape(1, d), b.reshape(1, d))


def _rope(x, cos, sin):
    half = x.shape[1] // 2
    x1, x2 = x[:, :half], x[:, half:]
    return jnp.concatenate([x1 * cos - x2 * sin, x1 * sin + x2 * cos], axis=1)


def _ret_kernel(dec_ref, q_ref, k_ref, v_ref, g_ref, cos_ref, sin_ref, dmat_ref, qdec_ref, kdec_ref, gn_ref,
                o_ref, state_ref, *, scale):
    h = pl.program_id(1)

    @pl.when(pl.program_id(2) == 0)
    def _():
        state_ref[...] = jnp.zeros_like(state_ref)

    cos, sin = cos_ref[...], sin_ref[...]
    qr = _rope(q_ref[...].astype(F32), cos, sin) * scale
    kr = _rope(k_ref[...].astype(F32), cos, sin)
    v = v_ref[...]
    sc = lax.dot_general(qr.astype(BF16), kr.astype(BF16), (((1,), (1,)), ((), ())),
                         preferred_element_type=F32) * dmat_ref[0]
    o = jnp.dot(sc.astype(BF16), v, preferred_element_type=F32)
    o = o + jnp.dot((qr * qdec_ref[0]).astype(BF16), state_ref[...].astype(BF16), preferred_element_type=F32)
    kd = (kr * kdec_ref[0]).astype(BF16)
    state_ref[...] = state_ref[...] * dec_ref[h] + lax.dot_general(
        kd, v, (((0,), (0,)), ((), ())), preferred_element_type=F32)
    mu = jnp.mean(o, axis=-1, keepdims=True)
    d = o - mu
    var = jnp.mean(d * d, axis=-1, keepdims=True)
    on = d * lax.rsqrt(var + LN_EPS) * gn_ref[...]
    g = g_ref[...].astype(F32)
    o_ref[...] = (g * jax.nn.sigmoid(g) * on).astype(BF16)


def _retention_tables(seq, dk, blk):
    pos = jnp.arange(seq, dtype=F32)
    inv_freq = ROPE_BASE ** (-jnp.arange(0, dk, 2, dtype=F32) / dk)
    ang = pos[:, None] * inv_freq[None, :]
    log_gamma = jnp.log(1.0 - jnp.exp2(-5.0 - jnp.arange(RET_HEADS, dtype=F32)))
    n = jnp.arange(blk, dtype=F32)
    dist = jnp.abs(n[:, None] - n[None, :])
    chunk = jnp.arange(blk) // CHUNK
    visible = (chunk[None, :] <= chunk[:, None]).astype(F32)
    dmat = jnp.exp(log_gamma[:, None, None] * dist) * visible
    qdec = jnp.broadcast_to(jnp.exp(log_gamma[:, None] * (n + 1.0))[:, :, None], (RET_HEADS, blk, dk))
    kdec = jnp.broadcast_to(jnp.exp(log_gamma[:, None] * (blk - 1.0 - n))[:, :, None], (RET_HEADS, blk, dk))
    dec = jnp.exp(log_gamma * blk)
    return jnp.cos(ang), jnp.sin(ang), dmat, qdec, kdec, dec


def _retention_core(proj, gn_g, batch, seq, d_model):
    dk = d_model // RET_HEADS
    dv = 2 * d_model // RET_HEADS
    blk = min(RET_BLOCK, seq)
    nblk = seq // blk
    cos, sin, dmat, qdec, kdec, dec = _retention_tables(seq, dk, blk)
    k_col0 = RET_HEADS
    v_col0 = 2 * RET_HEADS * dk // dv
    g_col0 = v_col0 + RET_HEADS
    head_tab = lambda b, h, s, dec: (h, 0, 0)
    grid_spec = pltpu.PrefetchScalarGridSpec(
        num_scalar_prefetch=1,
        grid=(batch, RET_HEADS, nblk),
        in_specs=[
            pl.BlockSpec((blk, dk), lambda b, h, s, dec: (b * nblk + s, h)),
            pl.BlockSpec((blk, dk), lambda b, h, s, dec: (b * nblk + s, k_col0 + h)),
            pl.BlockSpec((blk, dv), lambda b, h, s, dec: (b * nblk + s, v_col0 + h)),
            pl.BlockSpec((blk, dv), lambda b, h, s, dec: (b * nblk + s, g_col0 + h)),
            pl.BlockSpec((blk, dk // 2), lambda b, h, s, dec: (s, 0)),
            pl.BlockSpec((blk, dk // 2), lambda b, h, s, dec: (s, 0)),
            pl.BlockSpec((1, blk, blk), head_tab),
            pl.BlockSpec((1, blk, dk), head_tab),
            pl.BlockSpec((1, blk, dk), head_tab),
            pl.BlockSpec((1, dv), lambda b, h, s, dec: (0, h)),
        ],
        out_specs=pl.BlockSpec((blk, dv), lambda b, h, s, dec: (b * nblk + s, h)),
        scratch_shapes=[pltpu.VMEM((dk, dv), F32)],
    )
    return pl.pallas_call(
        functools.partial(_ret_kernel, scale=dk ** -0.5),
        grid_spec=grid_spec,
        out_shape=jax.ShapeDtypeStruct((batch * seq, RET_HEADS * dv), BF16),
        compiler_params=_params("parallel", "parallel", "arbitrary"),
        name="retention",
    )(dec, proj, proj, proj, proj, cos, sin, dmat, qdec, kdec, gn_g.reshape(1, -1))


def _sb_kernel(q_ref, k_ref, v_ref, o_ref, *, scale, dh):
    tq = q_ref.shape[0]
    n_heads = q_ref.shape[1] // dh
    tk = min(SB_TK, k_ref.shape[0])
    qi = pl.program_id(2)
    t_idx = qi * tq + lax.broadcasted_iota(jnp.int32, (tq, tk), 0)
    s_loc = lax.broadcasted_iota(jnp.int32, (tq, tk), 1)
    later = (lax.broadcasted_iota(jnp.int32, (tk, tk), 0) > lax.broadcasted_iota(jnp.int32, (tk, tk), 1)).astype(BF16)
    n_kb = (qi + 1) * tq // tk

    def one_head(q, kblk, vblk, causal, c, acc):
        z = lax.dot_general(q, kblk, (((1,), (1,)), ((), ())), preferred_element_type=F32) * scale
        log_beta = jnp.minimum(z, 0.0) - jnp.log(1.0 + jnp.exp(-jnp.abs(z)))
        log_1m = jnp.where(causal, log_beta - z, 0.0)
        hi = log_1m.astype(BF16)
        lo = (log_1m - hi.astype(F32)).astype(BF16)
        tail = (jnp.dot(hi, later, preferred_element_type=F32)
                + jnp.dot(lo, later, preferred_element_type=F32) + c)
        w = jnp.where(causal, jnp.exp(log_beta + tail), 0.0)
        acc = acc + jnp.dot(w.astype(BF16), vblk, preferred_element_type=F32)
        return c + jnp.sum(log_1m, axis=1, keepdims=True), acc

    def body(carry):
        it, _, state = carry
        kb = n_kb - 1 - it
        off = pl.multiple_of(kb * tk, tk)
        causal = (kb * tk + s_loc) < t_idx
        new_state = []
        c_max = None
        for u, (c, acc) in enumerate(state):
            lanes = slice(u * dh, (u + 1) * dh)
            c, acc = one_head(q_ref[:, lanes], k_ref[pl.ds(off, tk), lanes], v_ref[pl.ds(off, tk), lanes],
                              causal, c, acc)
            new_state.append((c, acc))
            c_max = jnp.max(c) if c_max is None else jnp.maximum(c_max, jnp.max(c))
        return it + 1, c_max, tuple(new_state)

    def more(carry):
        it, c_max, _ = carry
        return jnp.logical_and(it < n_kb, c_max > SB_EXP_ZERO)

    state = tuple((jnp.zeros((tq, 1), F32), jnp.zeros((tq, dh), F32)) for _ in range(n_heads))
    _, _, state = lax.while_loop(more, body, (jnp.int32(0), jnp.float32(0.0), state))
    for u, (_, acc) in enumerate(state):
        o_ref[:, u * dh:(u + 1) * dh] = acc.astype(o_ref.dtype)


def _stick_breaking_core(q, kv, batch, seq, d_model):
    dh = d_model // SB_HEADS
    tq = min(SB_TQ, seq)
    nq = seq // tq
    wide = SB_HEADS_PER_STEP * dh
    n_groups = SB_HEADS // SB_HEADS_PER_STEP
    return pl.pallas_call(
        functools.partial(_sb_kernel, scale=dh ** -0.5, dh=dh),
        grid=(batch, n_groups, nq),
        in_specs=[pl.BlockSpec((tq, wide), lambda b, h, i: (b * nq + i, h)),
                  pl.BlockSpec((seq, wide), lambda b, h, i: (b, h)),
                  pl.BlockSpec((seq, wide), lambda b, h, i: (b, n_groups + h))],
        out_specs=pl.BlockSpec((tq, wide), lambda b, h, i: (b * nq + i, h)),
        out_shape=jax.ShapeDtypeStruct((batch * seq, d_model), BF16),
        compiler_params=_params("parallel", "parallel", "parallel"),
        name="stick_breaking",
    )(q, kv, kv)


def _peer_score_kernel(xT_ref, wqT_ref, keys_ref, sT_ref):
    qT = jnp.dot(wqT_ref[...], xT_ref[...], preferred_element_type=F32)
    dq = keys_ref.shape[2]
    for g in range(keys_ref.shape[0]):
        sT_ref[g] = jnp.dot(keys_ref[g], qT[g * dq:(g + 1) * dq].astype(BF16), preferred_element_type=F32)


def _peer_scores(xT, wqT, keys):
    d, t = xT.shape
    groups, nk, dq = keys.shape
    tm = min(PEER_SCORE_TM, t)
    return pl.pallas_call(
        _peer_score_kernel,
        grid=(t // tm,),
        in_specs=[pl.BlockSpec((d, tm), lambda i: (0, i)),
                  pl.BlockSpec(wqT.shape, lambda i: (0, 0)),
                  pl.BlockSpec(keys.shape, lambda i: (0, 0, 0))],
        out_specs=pl.BlockSpec((groups, nk, tm), lambda i: (0, 0, i)),
        out_shape=jax.ShapeDtypeStruct((groups, nk, t), F32),
        compiler_params=_params("parallel"),
        name="peer_scores",
    )(xT, wqT, keys)


def _row_ids(shape):
    return lax.broadcasted_iota(jnp.int32, shape, 0).astype(F32)


def _argmax_mask(s, row, exact):
    m = jnp.max(s, axis=0, keepdims=True)
    if not exact:
        return m, s == m
    idx = jnp.min(jnp.where(s == m, row, float(s.shape[0])), axis=0, keepdims=True)
    return m, row == idx


def _topk_rank(s, vals_ref, exact):
    row = _row_ids(s.shape)
    rank = jnp.full(s.shape, float(PEER_TOPK), F32)
    for k in range(PEER_TOPK):
        m, sel = _argmax_mask(s, row, exact)
        rank = jnp.where(sel, float(k), rank)
        s = jnp.where(sel, -jnp.inf, s)
        vals_ref[k:k + 1, :] = m
    return rank


def _peer_topk_kernel(sT_ref, r2_ref, e2_ref, n_ref, a_ref, v1s_ref, v2s_ref, cs_ref, sels_ref):
    tm = sT_ref.shape[2]
    crow = _row_ids((_CAND_ROWS, tm))

    def rank_head(h, slot, exact):
        v1_ref, v2_ref, c_ref, sel_ref = (ref.at[slot] for ref in (v1s_ref, v2s_ref, cs_ref, sels_ref))
        s1 = sT_ref[2 * h]
        s2 = sT_ref[2 * h + 1]
        rank1 = _topk_rank(s1, v1_ref, exact)
        rank2 = _topk_rank(s2, v2_ref, exact)
        c_ref[...] = jnp.full(c_ref.shape, -jnp.inf, F32)
        for r, (p, q) in enumerate(_CAND):
            c_ref[r:r + 1, :] = v1_ref[p:p + 1, :] + v2_ref[q:q + 1, :]
        c = c_ref[...]
        chosen = jnp.zeros(c.shape, F32)
        m0 = v1_ref[0:1, :] + v2_ref[0:1, :]
        z = jnp.zeros((1, tm), F32)
        for _ in range(PEER_TOPK):
            m, sel = _argmax_mask(c, crow, exact)
            chosen = jnp.where(sel, 1.0, chosen)
            c = jnp.where(sel, -jnp.inf, c)
            z = z + jnp.exp(m - m0)
        sel_ref[...] = chosen
        n = jnp.zeros(s1.shape, F32)
        r0 = 0
        for p in range(PEER_TOPK):
            width = sum(1 for (pp, _) in _CAND if pp == p)
            n_p = jnp.sum(sel_ref[r0:r0 + width, :], axis=0, keepdims=True)
            n = jnp.where(rank1 == float(p), n_p, n)
            r0 += width
        r2_ref[h] = rank2.astype(r2_ref.dtype)
        e2_ref[h] = jnp.exp(s2 - v2_ref[0:1, :]).astype(e2_ref.dtype)
        n_ref[h] = n
        a_ref[h] = jnp.exp(s1 - v1_ref[0:1, :]) / z
        taken = jnp.maximum(
            jnp.maximum(jnp.sum(jnp.where(rank1 < PEER_TOPK, 1.0, 0.0), axis=0, keepdims=True),
                        jnp.sum(jnp.where(rank2 < PEER_TOPK, 1.0, 0.0), axis=0, keepdims=True)),
            jnp.sum(chosen, axis=0, keepdims=True))
        return jnp.max(taken)

    def head_group(hg, carry):
        heads = [hg * PEER_TOPK_HEADS_PER_STEP + u for u in range(PEER_TOPK_HEADS_PER_STEP)]
        most_taken = [rank_head(h, u, exact=False) for u, h in enumerate(heads)]
        for u, h in enumerate(heads):
            @pl.when(most_taken[u] > PEER_TOPK)
            def _():
                rank_head(h, u, exact=True)

        return carry

    lax.fori_loop(0, r2_ref.shape[0] // PEER_TOPK_HEADS_PER_STEP, head_group, 0)


def _peer_topk(sT):
    groups, nk, t = sT.shape
    heads = groups // 2
    tm = min(PEER_TOPK_TM, t)
    out = pl.BlockSpec((heads, nk, tm), lambda i: (0, 0, i))
    wide = jax.ShapeDtypeStruct((heads, nk, t), F32)
    narrow = jax.ShapeDtypeStruct((heads, nk, t), BF16)
    return pl.pallas_call(
        _peer_topk_kernel,
        grid=(t // tm,),
        in_specs=[pl.BlockSpec((groups, nk, tm), lambda i: (0, 0, i))],
        out_specs=[out, out, out, out],
        out_shape=[narrow, narrow, wide, wide],
        scratch_shapes=[pltpu.VMEM((PEER_TOPK_HEADS_PER_STEP, rows, tm), F32)
                        for rows in (PEER_TOPK, PEER_TOPK, _CAND_ROWS, _CAND_ROWS)],
        compiler_params=_params("parallel"),
        name="peer_topk",
    )(sT)


def _peer_dense_kernel(xT_ref, u_ref, vT_ref, r2_ref, e2_ref, n_ref, a_ref, y_ref, act_ref):
    e = pl.program_id(1)
    heads, nk, _ = r2_ref.shape
    ni = u_ref.shape[0] // nk

    @pl.when(e == 0)
    def _():
        y_ref[...] = jnp.zeros_like(y_ref)

    zero = jnp.zeros((), e2_ref.dtype)
    group = ni // PEER_GROUPS
    for gi in range(PEER_GROUPS):
        rows = slice(gi * group * nk, (gi + 1) * group * nk)
        hT = jnp.dot(u_ref[rows, :], xT_ref[...], preferred_element_type=F32)
        for ii in range(group):
            i = e * ni + gi * group + ii
            w = None
            for h in range(heads):
                n = n_ref[h, pl.ds(i, 1), :].astype(r2_ref.dtype)
                a = a_ref[h, pl.ds(i, 1), :].astype(e2_ref.dtype)
                term = jnp.where(r2_ref[h] < n, e2_ref[h], zero) * a
                w = term if w is None else w + term
            g = jax.nn.gelu(hT[ii * nk:(ii + 1) * nk]).astype(w.dtype)
            act_ref[(gi * group + ii) * nk:(gi * group + ii + 1) * nk, :] = g * w
        y_ref[...] += jnp.dot(vT_ref[:, rows], act_ref[rows, :], preferred_element_type=F32)


def _peer_dense(xT, u, vT, r2, e2, n, a):
    d, t = xT.shape
    n_exp = u.shape[0]
    heads, nk, _ = r2.shape
    tm = min(PEER_TM, t)
    te = PEER_NI * nk
    sel = pl.BlockSpec((heads, nk, tm), lambda i, e: (0, 0, i))
    return pl.pallas_call(
        _peer_dense_kernel,
        grid=(t // tm, n_exp // te),
        in_specs=[pl.BlockSpec((d, tm), lambda i, e: (0, i)),
                  pl.BlockSpec((te, d), lambda i, e: (e, 0)),
                  pl.BlockSpec((d, te), lambda i, e: (0, e)),
                  sel, sel, sel, sel],
        out_specs=pl.BlockSpec((d, tm), lambda i, e: (0, i)),
        out_shape=jax.ShapeDtypeStruct((d, t), F32),
        scratch_shapes=[pltpu.VMEM((te, tm), BF16)],
        compiler_params=_params("parallel", "arbitrary"),
        name="peer_dense",
    )(xT, u, vT, r2, e2, n, a)


def _peer(xb, wq, sub_keys, u_tab, v_tab):
    xT = xb.T
    keys = sub_keys.reshape(-1, sub_keys.shape[-2], sub_keys.shape[-1]).astype(BF16)
    sT = _peer_scores(xT, wq.T.astype(BF16), keys)
    r2, e2, n, a = _peer_topk(sT)
    return _peer_dense(xT, u_tab.astype(BF16), v_tab.T.astype(BF16), r2, e2, n, a)


def kernel(x, ret_w_in, ret_gn_g, ret_w_out, kv_w, sb_wq, sb_w_out, peer_wq, peer_sub_keys, peer_u, peer_v,
           ln_g, ln_b):
    batch, seq, d_model = x.shape
    depth = ln_g.shape[0]
    n_a = ret_w_in.shape[0]
    alpha = (2.0 * depth) ** 0.25
    xf = x.reshape(batch * seq, d_model)
    xb = xf.astype(BF16)
    kv = None
    for l in range(depth):
        if l < n_a:
            proj = _matmul(xb, ret_w_in[l].astype(BF16), BF16)
            o = _retention_core(proj, ret_gn_g[l], batch, seq, d_model)
            w_out = ret_w_out[l]
        else:
            q = _matmul(xb, sb_wq[l - n_a].astype(BF16), BF16)
            o = _stick_breaking_core(q, kv, batch, seq, d_model)
            w_out = sb_w_out[l - n_a]
        xf, xb = _matmul_ln(o, w_out.astype(BF16), xf, ln_g[l, 0], ln_b[l, 0], alpha)
        yT = _peer(xb, peer_wq[l], peer_sub_keys[l], peer_u[l], peer_v[l])
        xf, xb = _ln_res_t(xf, yT, ln_g[l, 1], ln_b[l, 1], alpha)
        if l == n_a - 1:
            kv = _matmul(xb, kv_w.astype(BF16), BF16)
    return xf.reshape(batch, seq, d_model)
```

```python
import functools

import jax
import jax.numpy as jnp
from jax import lax
from jax.experimental import pallas as pl
from jax.experimental.pallas import tpu as pltpu

F32 = jnp.float32
BF16 = jnp.bfloat16

LN_EPS = 1e-5
ROPE_BASE = 10000.0
CHUNK = 64
RET_HEADS = 8
SB_HEADS = 16
PEER_HEADS = 8
PEER_TOPK = 16
N_KEYS = 128

MIB = 1024 * 1024
VMEM_LIMIT_BYTES = 56 * MIB

MM_TM, MM_TN = 1024, 1024
MM_LN_TM, MM_LN_TK = 512, 2048
LN_TM = 256
RET_BLOCK = 512
SB_TQ = SB_TK = 256
SB_HEADS_PER_STEP = 2
SB_EXP_ZERO = -104.0
PEER_SCORE_TM = 512
PEER_TOPK_TM = 128
PEER_TOPK_HEADS_PER_STEP = 2
PEER_TM = 512
PEER_NI = 8
PEER_GROUPS = 2

_CAND = [(p, q) for p in range(PEER_TOPK) for q in range(PEER_TOPK) if (p + 1) * (q + 1) <= PEER_TOPK]
_CAND_ROWS = -(-len(_CAND) // 8) * 8


def _params(*sem):
    return pltpu.CompilerParams(dimension_semantics=sem, vmem_limit_bytes=VMEM_LIMIT_BYTES)


def _layer_norm(z, g, b):
    mu = jnp.mean(z, axis=-1, keepdims=True)
    d = z - mu
    var = jnp.mean(d * d, axis=-1, keepdims=True)
    return d * lax.rsqrt(var + LN_EPS) * g + b


def _mm_kernel(a_ref, b_ref, o_ref):
    o_ref[...] = jnp.dot(a_ref[...], b_ref[...], preferred_element_type=F32).astype(o_ref.dtype)


def _matmul(a, b, out_dtype):
    m, k = a.shape
    n = b.shape[1]
    tm, tn = min(MM_TM, m), min(MM_TN, n)
    return pl.pallas_call(
        _mm_kernel,
        grid=(m // tm, n // tn),
        in_specs=[pl.BlockSpec((tm, k), lambda i, j: (i, 0)),
                  pl.BlockSpec((k, tn), lambda i, j: (0, j))],
        out_specs=pl.BlockSpec((tm, tn), lambda i, j: (i, j)),
        out_shape=jax.ShapeDtypeStruct((m, n), out_dtype),
        compiler_params=_params("parallel", "parallel"),
        name="mm",
    )(a, b)


def _mm_ln_kernel(a_ref, b_ref, x_ref, g_ref, beta_ref, of_ref, ob_ref, acc_ref, *, alpha):
    k = pl.program_id(1)

    @pl.when(k == 0)
    def _():
        acc_ref[...] = jnp.zeros_like(acc_ref)

    acc_ref[...] += jnp.dot(a_ref[...], b_ref[...], preferred_element_type=F32)

    @pl.when(k == pl.num_programs(1) - 1)
    def _():
        o = _layer_norm(alpha * x_ref[...] + acc_ref[...], g_ref[...], beta_ref[...])
        of_ref[...] = o
        ob_ref[...] = o.astype(ob_ref.dtype)


def _matmul_ln(a, b, x, g, beta, alpha):
    m, k = a.shape
    d = b.shape[1]
    tm, tk = min(MM_LN_TM, m), min(MM_LN_TK, k)
    row = pl.BlockSpec((tm, d), lambda i, j: (i, 0))
    vec = pl.BlockSpec((1, d), lambda i, j: (0, 0))
    return pl.pallas_call(
        functools.partial(_mm_ln_kernel, alpha=alpha),
        grid=(m // tm, k // tk),
        in_specs=[pl.BlockSpec((tm, tk), lambda i, j: (i, j)),
                  pl.BlockSpec((tk, d), lambda i, j: (j, 0)),
                  row, vec, vec],
        out_specs=[row, row],
        out_shape=[jax.ShapeDtypeStruct((m, d), F32), jax.ShapeDtypeStruct((m, d), BF16)],
        scratch_shapes=[pltpu.VMEM((tm, d), F32)],
        compiler_params=_params("parallel", "arbitrary"),
        name="mm_ln",
    )(a, b, x, g.reshape(1, d), beta.reshape(1, d))


def _ln_t_kernel(x_ref, yT_ref, g_ref, b_ref, of_ref, ob_ref, *, alpha):
    o = _layer_norm(alpha * x_ref[...] + yT_ref[...].T, g_ref[...], b_ref[...])
    of_ref[...] = o
    ob_ref[...] = o.astype(ob_ref.dtype)


def _ln_res_t(x, yT, g, b, alpha):
    t, d = x.shape
    tm = min(LN_TM, t)
    row = pl.BlockSpec((tm, d), lambda i: (i, 0))
    vec = pl.BlockSpec((1, d), lambda i: (0, 0))
    return pl.pallas_call(
        functools.partial(_ln_t_kernel, alpha=alpha),
        grid=(t // tm,),
        in_specs=[row, pl.BlockSpec((d, tm), lambda i: (0, i)), vec, vec],
        out_specs=[row, row],
        out_shape=[jax.ShapeDtypeStruct((t, d), F32), jax.ShapeDtypeStruct((t, d), BF16)],
        compiler_params=_params("parallel"),
        name="ln_res",
    )(x, yT, g.reshape(1, d), b.reshape(1, d))


def _rope(x, cos, sin):
    half = x.shape[1] // 2
    x1, x2 = x[:, :half], x[:, half:]
    return jnp.concatenate([x1 * cos - x2 * sin, x1 * sin + x2 * cos], axis=1)


def _ret_kernel(dec_ref, q_ref, k_ref, v_ref, g_ref, cos_ref, sin_ref, dmat_ref, qdec_ref, kdec_ref, gn_ref,
                o_ref, state_ref, *, scale):
    h = pl.program_id(1)

    @pl.when(pl.program_id(2) == 0)
    def _():
        state_ref[...] = jnp.zeros_like(state_ref)

    cos, sin = cos_ref[...], sin_ref[...]
    qr = _rope(q_ref[...].astype(F32), cos, sin) * scale
    kr = _rope(k_ref[...].astype(F32), cos, sin)
    v = v_ref[...]
    sc = lax.dot_general(qr.astype(BF16), kr.astype(BF16), (((1,), (1,)), ((), ())),
                         preferred_element_type=F32) * dmat_ref[0]
    o = jnp.dot(sc.astype(BF16), v, preferred_element_type=F32)
    o = o + jnp.dot((qr * qdec_ref[0]).astype(BF16), state_ref[...].astype(BF16), preferred_element_type=F32)
    kd = (kr * kdec_ref[0]).astype(BF16)
    state_ref[...] = state_ref[...] * dec_ref[h] + lax.dot_general(
        kd, v, (((0,), (0,)), ((), ())), preferred_element_type=F32)
    mu = jnp.mean(o, axis=-1, keepdims=True)
    d = o - mu
    var = jnp.mean(d * d, axis=-1, keepdims=True)
    on = d * lax.rsqrt(var + LN_EPS) * gn_ref[...]
    g = g_ref[...].astype(F32)
    o_ref[...] = (g * jax.nn.sigmoid(g) * on).astype(BF16)


def _retention_tables(seq, dk, blk):
    pos = jnp.arange(seq, dtype=F32)
    inv_freq = ROPE_BASE ** (-jnp.arange(0, dk, 2, dtype=F32) / dk)
    ang = pos[:, None] * inv_freq[None, :]
    log_gamma = jnp.log(1.0 - jnp.exp2(-5.0 - jnp.arange(RET_HEADS, dtype=F32)))
    n = jnp.arange(blk, dtype=F32)
    dist = jnp.abs(n[:, None] - n[None, :])
    chunk = jnp.arange(blk) // CHUNK
    visible = (chunk[None, :] <= chunk[:, None]).astype(F32)
    dmat = jnp.exp(log_gamma[:, None, None] * dist) * visible
    qdec = jnp.broadcast_to(jnp.exp(log_gamma[:, None] * (n + 1.0))[:, :, None], (RET_HEADS, blk, dk))
    kdec = jnp.broadcast_to(jnp.exp(log_gamma[:, None] * (blk - 1.0 - n))[:, :, None], (RET_HEADS, blk, dk))
    dec = jnp.exp(log_gamma * blk)
    return jnp.cos(ang), jnp.sin(ang), dmat, qdec, kdec, dec


def _retention_core(proj, gn_g, batch, seq, d_model):
    dk = d_model // RET_HEADS
    dv = 2 * d_model // RET_HEADS
    blk = min(RET_BLOCK, seq)
    nblk = seq // blk
    cos, sin, dmat, qdec, kdec, dec = _retention_tables(seq, dk, blk)
    k_col0 = RET_HEADS
    v_col0 = 2 * RET_HEADS * dk // dv
    g_col0 = v_col0 + RET_HEADS
    head_tab = lambda b, h, s, dec: (h, 0, 0)
    grid_spec = pltpu.PrefetchScalarGridSpec(
        num_scalar_prefetch=1,
        grid=(batch, RET_HEADS, nblk),
        in_specs=[
            pl.BlockSpec((blk, dk), lambda b, h, s, dec: (b * nblk + s, h)),
            pl.BlockSpec((blk, dk), lambda b, h, s, dec: (b * nblk + s, k_col0 + h)),
            pl.BlockSpec((blk, dv), lambda b, h, s, dec: (b * nblk + s, v_col0 + h)),
            pl.BlockSpec((blk, dv), lambda b, h, s, dec: (b * nblk + s, g_col0 + h)),
            pl.BlockSpec((blk, dk // 2), lambda b, h, s, dec: (s, 0)),
            pl.BlockSpec((blk, dk // 2), lambda b, h, s, dec: (s, 0)),
            pl.BlockSpec((1, blk, blk), head_tab),
            pl.BlockSpec((1, blk, dk), head_tab),
            pl.BlockSpec((1, blk, dk), head_tab),
            pl.BlockSpec((1, dv), lambda b, h, s, dec: (0, h)),
        ],
        out_specs=pl.BlockSpec((blk, dv), lambda b, h, s, dec: (b * nblk + s, h)),
        scratch_shapes=[pltpu.VMEM((dk, dv), F32)],
    )
    return pl.pallas_call(
        functools.partial(_ret_kernel, scale=dk ** -0.5),
        grid_spec=grid_spec,
        out_shape=jax.ShapeDtypeStruct((batch * seq, RET_HEADS * dv), BF16),
        compiler_params=_params("parallel", "parallel", "arbitrary"),
        name="retention",
    )(dec, proj, proj, proj, proj, cos, sin, dmat, qdec, kdec, gn_g.reshape(1, -1))


def _sb_kernel(q_ref, k_ref, v_ref, o_ref, *, scale, dh):
    tq = q_ref.shape[0]
    n_heads = q_ref.shape[1] // dh
    tk = min(SB_TK, k_ref.shape[0])
    qi = pl.program_id(2)
    t_idx = qi * tq + lax.broadcasted_iota(jnp.int32, (tq, tk), 0)
    s_loc = lax.broadcasted_iota(jnp.int32, (tq, tk), 1)
    later = (lax.broadcasted_iota(jnp.int32, (tk, tk), 0) > lax.broadcasted_iota(jnp.int32, (tk, tk), 1)).astype(BF16)
    n_kb = (qi + 1) * tq // tk

    def one_head(q, kblk, vblk, causal, c, acc):
        z = lax.dot_general(q, kblk, (((1,), (1,)), ((), ())), preferred_element_type=F32) * scale
        log_beta = jnp.minimum(z, 0.0) - jnp.log(1.0 + jnp.exp(-jnp.abs(z)))
        log_1m = jnp.where(causal, log_beta - z, 0.0)
        hi = log_1m.astype(BF16)
        lo = (log_1m - hi.astype(F32)).astype(BF16)
        tail = (jnp.dot(hi, later, preferred_element_type=F32)
                + jnp.dot(lo, later, preferred_element_type=F32) + c)
        w = jnp.where(causal, jnp.exp(log_beta + tail), 0.0)
        acc = acc + jnp.dot(w.astype(BF16), vblk, preferred_element_type=F32)
        return c + jnp.sum(log_1m, axis=1, keepdims=True), acc

    def body(carry):
        it, _, state = carry
        kb = n_kb - 1 - it
        off = pl.multiple_of(kb * tk, tk)
        causal = (kb * tk + s_loc) < t_idx
        new_state = []
        c_max = None
        for u, (c, acc) in enumerate(state):
            lanes = slice(u * dh, (u + 1) * dh)
            c, acc = one_head(q_ref[:, lanes], k_ref[pl.ds(off, tk), lanes], v_ref[pl.ds(off, tk), lanes],
                              causal, c, acc)
            new_state.append((c, acc))
            c_max = jnp.max(c) if c_max is None else jnp.maximum(c_max, jnp.max(c))
        return it + 1, c_max, tuple(new_state)

    def more(carry):
        it, c_max, _ = carry
        return jnp.logical_and(it < n_kb, c_max > SB_EXP_ZERO)

    state = tuple((jnp.zeros((tq, 1), F32), jnp.zeros((tq, dh), F32)) for _ in range(n_heads))
    _, _, state = lax.while_loop(more, body, (jnp.int32(0), jnp.float32(0.0), state))
    for u, (_, acc) in enumerate(state):
        o_ref[:, u * dh:(u + 1) * dh] = acc.astype(o_ref.dtype)


def _stick_breaking_core(q, kv, batch, seq, d_model):
    dh = d_model // SB_HEADS
    tq = min(SB_TQ, seq)
    nq = seq // tq
    wide = SB_HEADS_PER_STEP * dh
    n_groups = SB_HEADS // SB_HEADS_PER_STEP
    return pl.pallas_call(
        functools.partial(_sb_kernel, scale=dh ** -0.5, dh=dh),
        grid=(batch, n_groups, nq),
        in_specs=[pl.BlockSpec((tq, wide), lambda b, h, i: (b * nq + i, h)),
                  pl.BlockSpec((seq, wide), lambda b, h, i: (b, h)),
                  pl.BlockSpec((seq, wide), lambda b, h, i: (b, n_groups + h))],
        out_specs=pl.BlockSpec((tq, wide), lambda b, h, i: (b * nq + i, h)),
        out_shape=jax.ShapeDtypeStruct((batch * seq, d_model), BF16),
        compiler_params=_params("parallel", "parallel", "parallel"),
        name="stick_breaking",
    )(q, kv, kv)


def _peer_score_kernel(xT_ref, wqT_ref, keys_ref, sT_ref):
    qT = jnp.dot(wqT_ref[...], xT_ref[...], preferred_element_type=F32)
    dq = keys_ref.shape[2]
    for g in range(keys_ref.shape[0]):
        sT_ref[g] = jnp.dot(keys_ref[g], qT[g * dq:(g + 1) * dq].astype(BF16), preferred_element_type=F32)


def _peer_scores(xT, wqT, keys):
    d, t = xT.shape
    groups, nk, dq = keys.shape
    tm = min(PEER_SCORE_TM, t)
    return pl.pallas_call(
        _peer_score_kernel,
        grid=(t // tm,),
        in_specs=[pl.BlockSpec((d, tm), lambda i: (0, i)),
                  pl.BlockSpec(wqT.shape, lambda i: (0, 0)),
                  pl.BlockSpec(keys.shape, lambda i: (0, 0, 0))],
        out_specs=pl.BlockSpec((groups, nk, tm), lambda i: (0, 0, i)),
        out_shape=jax.ShapeDtypeStruct((groups, nk, t), F32),
        compiler_params=_params("parallel"),
        name="peer_scores",
    )(xT, wqT, keys)


def _row_ids(shape):
    return lax.broadcasted_iota(jnp.int32, shape, 0).astype(F32)


def _argmax_mask(s, row, exact):
    m = jnp.max(s, axis=0, keepdims=True)
    if not exact:
        return m, s == m
    idx = jnp.min(jnp.where(s == m, row, float(s.shape[0])), axis=0, keepdims=True)
    return m, row == idx


def _topk_rank(s, vals_ref, exact):
    row = _row_ids(s.shape)
    rank = jnp.full(s.shape, float(PEER_TOPK), F32)
    for k in range(PEER_TOPK):
        m, sel = _argmax_mask(s, row, exact)
        rank = jnp.where(sel, float(k), rank)
        s = jnp.where(sel, -jnp.inf, s)
        vals_ref[k:k + 1, :] = m
    return rank


def _peer_topk_kernel(sT_ref, r2_ref, e2_ref, n_ref, a_ref, v1s_ref, v2s_ref, cs_ref, sels_ref):
    tm = sT_ref.shape[2]
    crow = _row_ids((_CAND_ROWS, tm))

    def rank_head(h, slot, exact):
        v1_ref, v2_ref, c_ref, sel_ref = (ref.at[slot] for ref in (v1s_ref, v2s_ref, cs_ref, sels_ref))
        s1 = sT_ref[2 * h]
        s2 = sT_ref[2 * h + 1]
        rank1 = _topk_rank(s1, v1_ref, exact)
        rank2 = _topk_rank(s2, v2_ref, exact)
        c_ref[...] = jnp.full(c_ref.shape, -jnp.inf, F32)
        for r, (p, q) in enumerate(_CAND):
            c_ref[r:r + 1, :] = v1_ref[p:p + 1, :] + v2_ref[q:q + 1, :]
        c = c_ref[...]
        chosen = jnp.zeros(c.shape, F32)
        m0 = v1_ref[0:1, :] + v2_ref[0:1, :]
        z = jnp.zeros((1, tm), F32)
        for _ in range(PEER_TOPK):
            m, sel = _argmax_mask(c, crow, exact)
            chosen = jnp.where(sel, 1.0, chosen)
            c = jnp.where(sel, -jnp.inf, c)
            z = z + jnp.exp(m - m0)
        sel_ref[...] = chosen
        n = jnp.zeros(s1.shape, F32)
        r0 = 0
        for p in range(PEER_TOPK):
            width = sum(1 for (pp, _) in _CAND if pp == p)
            n_p = jnp.sum(sel_ref[r0:r0 + width, :], axis=0, keepdims=True)
            n = jnp.where(rank1 == float(p), n_p, n)
            r0 += width
        r2_ref[h] = rank2.astype(r2_ref.dtype)
        e2_ref[h] = jnp.exp(s2 - v2_ref[0:1, :]).astype(e2_ref.dtype)
        n_ref[h] = n
        a_ref[h] = jnp.exp(s1 - v1_ref[0:1, :]) / z
        taken = jnp.maximum(
            jnp.maximum(jnp.sum(jnp.where(rank1 < PEER_TOPK, 1.0, 0.0), axis=0, keepdims=True),
                        jnp.sum(jnp.where(rank2 < PEER_TOPK, 1.0, 0.0), axis=0, keepdims=True)),
            jnp.sum(chosen, axis=0, keepdims=True))
        return jnp.max(taken)

    def head_group(hg, carry):
        heads = [hg * PEER_TOPK_HEADS_PER_STEP + u for u in range(PEER_TOPK_HEADS_PER_STEP)]
        most_taken = [rank_head(h, u, exact=False) for u, h in enumerate(heads)]
        for u, h in enumerate(heads):
            @pl.when(most_taken[u] > PEER_TOPK)
            def _():
                rank_head(h, u, exact=True)

        return carry

    lax.fori_loop(0, r2_ref.shape[0] // PEER_TOPK_HEADS_PER_STEP, head_group, 0)


def _peer_topk(sT):
    groups, nk, t = sT.shape
    heads = groups // 2
    tm = min(PEER_TOPK_TM, t)
    out = pl.BlockSpec((heads, nk, tm), lambda i: (0, 0, i))
    wide = jax.ShapeDtypeStruct((heads, nk, t), F32)
    narrow = jax.ShapeDtypeStruct((heads, nk, t), BF16)
    return pl.pallas_call(
        _peer_topk_kernel,
        grid=(t // tm,),
        in_specs=[pl.BlockSpec((groups, nk, tm), lambda i: (0, 0, i))],
        out_specs=[out, out, out, out],
        out_shape=[narrow, narrow, wide, wide],
        scratch_shapes=[pltpu.VMEM((PEER_TOPK_HEADS_PER_STEP, rows, tm), F32)
                        for rows in (PEER_TOPK, PEER_TOPK, _CAND_ROWS, _CAND_ROWS)],
        compiler_params=_params("parallel"),
        name="peer_topk",
    )(sT)


def _peer_dense_kernel(xT_ref, u_ref, vT_ref, r2_ref, e2_ref, n_ref, a_ref, y_ref, act_ref):
    e = pl.program_id(1)
    heads, nk, _ = r2_ref.shape
    ni = u_ref.shape[0] // nk

    @pl.when(e == 0)
    def _():
        y_ref[...] = jnp.zeros_like(y_ref)

    zero = jnp.zeros((), e2_ref.dtype)
    group = ni // PEER_GROUPS
    for gi in range(PEER_GROUPS):
        rows = slice(gi * group * nk, (gi + 1) * group * nk)
        hT = jnp.dot(u_ref[rows, :], xT_ref[...], preferred_element_type=F32)
        for ii in range(group):
            i = e * ni + gi * group + ii
            w = None
            for h in range(heads):
                n = n_ref[h, pl.ds(i, 1), :].astype(r2_ref.dtype)
                a = a_ref[h, pl.ds(i, 1), :].astype(e2_ref.dtype)
                term = jnp.where(r2_ref[h] < n, e2_ref[h], zero) * a
                w = term if w is None else w + term
            g = jax.nn.gelu(hT[ii * nk:(ii + 1) * nk].astype(w.dtype))
            act_ref[(gi * group + ii) * nk:(gi * group + ii + 1) * nk, :] = g * w
        y_ref[...] += jnp.dot(vT_ref[:, rows], act_ref[rows, :], preferred_element_type=F32)


def _peer_dense(xT, u, vT, r2, e2, n, a):
    d, t = xT.shape
    n_exp = u.shape[0]
    heads, nk, _ = r2.shape
    tm = min(PEER_TM, t)
    te = PEER_NI * nk
    sel = pl.BlockSpec((heads, nk, tm), lambda i, e: (0, 0, i))
    return pl.pallas_call(
        _peer_dense_kernel,
        grid=(t // tm, n_exp // te),
        in_specs=[pl.BlockSpec((d, tm), lambda i, e: (0, i)),
                  pl.BlockSpec((te, d), lambda i, e: (e, 0)),
                  pl.BlockSpec((d, te), lambda i, e: (0, e)),
                  sel, sel, sel, sel],
        out_specs=pl.BlockSpec((d, tm), lambda i, e: (0, i)),
        out_shape=jax.ShapeDtypeStruct((d, t), F32),
        scratch_shapes=[pltpu.VMEM((te, tm), BF16)],
        compiler_params=_params("parallel", "arbitrary"),
        name="peer_dense",
    )(xT, u, vT, r2, e2, n, a)


def _peer(xb, wq, sub_keys, u_tab, v_tab):
    xT = xb.T
    keys = sub_keys.reshape(-1, sub_keys.shape[-2], sub_keys.shape[-1]).astype(BF16)
    sT = _peer_scores(xT, wq.T.astype(BF16), keys)
    r2, e2, n, a = _peer_topk(sT)
    return _peer_dense(xT, u_tab.astype(BF16), v_tab.T.astype(BF16), r2, e2, n, a)


def kernel(x, ret_w_in, ret_gn_g, ret_w_out, kv_w, sb_wq, sb_w_out, peer_wq, peer_sub_keys, peer_u, peer_v,
           ln_g, ln_b):
    batch, seq, d_model = x.shape
    depth = ln_g.shape[0]
    n_a = ret_w_in.shape[0]
    alpha = (2.0 * depth) ** 0.25
    xf = x.reshape(batch * seq, d_model)
    xb = xf.astype(BF16)
    kv = None
    for l in range(depth):
        if l < n_a:
            proj = _matmul(xb, ret_w_in[l].astype(BF16), BF16)
            o = _retention_core(proj, ret_gn_g[l], batch, seq, d_model)
            w_out = ret_w_out[l]
        else:
            q = _matmul(xb, sb_wq[l - n_a].astype(BF16), BF16)
            o = _stick_breaking_core(q, kv, batch, seq, d_model)
            w_out = sb_w_out[l - n_a]
        xf, xb = _matmul_ln(o, w_out.astype(BF16), xf, ln_g[l, 0], ln_b[l, 0], alpha)
        yT = _peer(xb, peer_wq[l], peer_sub_keys[l], peer_u[l], peer_v[l])
        xf, xb = _ln_res_t(xf, yT, ln_g[l, 1], ln_b[l, 1], alpha)
        if l == n_a - 1:
            kv = _matmul(xb, kv_w.astype(BF16), BF16)
    return xf.reshape(batch, seq, d_model)
```

```python
import functools

import jax
import jax.numpy as jnp
from jax import lax
from jax.experimental import pallas as pl
from jax.experimental.pallas import tpu as pltpu

F32 = jnp.float32
BF16 = jnp.bfloat16

LN_EPS = 1e-5
ROPE_BASE = 10000.0
CHUNK = 64
RET_HEADS = 8
SB_HEADS = 16
PEER_HEADS = 8
PEER_TOPK = 16
N_KEYS = 128

MIB = 1024 * 1024
VMEM_LIMIT_BYTES = 56 * MIB

MM_TM, MM_TN = 1024, 1024
MM_LN_TM, MM_LN_TK = 512, 2048
LN_TM = 256
RET_BLOCK = 512
SB_TQ = SB_TK = 256
SB_HEADS_PER_STEP = 4
SB_EXP_ZERO = -104.0
PEER_SCORE_TM = 512
PEER_TOPK_TM = 128
PEER_TOPK_HEADS_PER_STEP = 2
PEER_TM = 512
PEER_NI = 8
PEER_GROUPS = 2

_CAND = [(p, q) for p in range(PEER_TOPK) for q in range(PEER_TOPK) if (p + 1) * (q + 1) <= PEER_TOPK]
_CAND_ROWS = -(-len(_CAND) // 8) * 8


def _params(*sem):
    return pltpu.CompilerParams(dimension_semantics=sem, vmem_limit_bytes=VMEM_LIMIT_BYTES)


def _layer_norm(z, g, b):
    mu = jnp.mean(z, axis=-1, keepdims=True)
    d = z - mu
    var = jnp.mean(d * d, axis=-1, keepdims=True)
    return d * lax.rsqrt(var + LN_EPS) * g + b


def _mm_kernel(a_ref, b_ref, o_ref):
    o_ref[...] = jnp.dot(a_ref[...], b_ref[...], preferred_element_type=F32).astype(o_ref.dtype)


def _matmul(a, b, out_dtype):
    m, k = a.shape
    n = b.shape[1]
    tm, tn = min(MM_TM, m), min(MM_TN, n)
    return pl.pallas_call(
        _mm_kernel,
        grid=(m // tm, n // tn),
        in_specs=[pl.BlockSpec((tm, k), lambda i, j: (i, 0)),
                  pl.BlockSpec((k, tn), lambda i, j: (0, j))],
        out_specs=pl.BlockSpec((tm, tn), lambda i, j: (i, j)),
        out_shape=jax.ShapeDtypeStruct((m, n), out_dtype),
        compiler_params=_params("parallel", "parallel"),
        name="mm",
    )(a, b)


def _mm_ln_kernel(a_ref, b_ref, x_ref, g_ref, beta_ref, of_ref, ob_ref, acc_ref, *, alpha):
    k = pl.program_id(1)

    @pl.when(k == 0)
    def _():
        acc_ref[...] = jnp.zeros_like(acc_ref)

    acc_ref[...] += jnp.dot(a_ref[...], b_ref[...], preferred_element_type=F32)

    @pl.when(k == pl.num_programs(1) - 1)
    def _():
        o = _layer_norm(alpha * x_ref[...] + acc_ref[...], g_ref[...], beta_ref[...])
        of_ref[...] = o
        ob_ref[...] = o.astype(ob_ref.dtype)


def _matmul_ln(a, b, x, g, beta, alpha):
    m, k = a.shape
    d = b.shape[1]
    tm, tk = min(MM_LN_TM, m), min(MM_LN_TK, k)
    row = pl.BlockSpec((tm, d), lambda i, j: (i, 0))
    vec = pl.BlockSpec((1, d), lambda i, j: (0, 0))
    return pl.pallas_call(
        functools.partial(_mm_ln_kernel, alpha=alpha),
        grid=(m // tm, k // tk),
        in_specs=[pl.BlockSpec((tm, tk), lambda i, j: (i, j)),
                  pl.BlockSpec((tk, d), lambda i, j: (j, 0)),
                  row, vec, vec],
        out_specs=[row, row],
        out_shape=[jax.ShapeDtypeStruct((m, d), F32), jax.ShapeDtypeStruct((m, d), BF16)],
        scratch_shapes=[pltpu.VMEM((tm, d), F32)],
        compiler_params=_params("parallel", "arbitrary"),
        name="mm_ln",
    )(a, b, x, g.reshape(1, d), beta.reshape(1, d))


def _ln_t_kernel(x_ref, yT_ref, g_ref, b_ref, of_ref, ob_ref, *, alpha):
    o = _layer_norm(alpha * x_ref[...] + yT_ref[...].T, g_ref[...], b_ref[...])
    of_ref[...] = o
    ob_ref[...] = o.astype(ob_ref.dtype)


def _ln_res_t(x, yT, g, b, alpha):
    t, d = x.shape
    tm = min(LN_TM, t)
    row = pl.BlockSpec((tm, d), lambda i: (i, 0))
    vec = pl.BlockSpec((1, d), lambda i: (0, 0))
    return pl.pallas_call(
        functools.partial(_ln_t_kernel, alpha=alpha),
        grid=(t // tm,),
        in_specs=[row, pl.BlockSpec((d, tm), lambda i: (0, i)), vec, vec],
        out_specs=[row, row],
        out_shape=[jax.ShapeDtypeStruct((t, d), F32), jax.ShapeDtypeStruct((t, d), BF16)],
        compiler_params=_params("parallel"),
        name="ln_res",
    )(x, yT, g.reshape(1, d), b.reshape(1, d))


def _rope(x, cos, sin):
    half = x.shape[1] // 2
    x1, x2 = x[:, :half], x[:, half:]
    return jnp.concatenate([x1 * cos - x2 * sin, x1 * sin + x2 * cos], axis=1)


def _ret_kernel(dec_ref, q_ref, k_ref, v_ref, g_ref, cos_ref, sin_ref, dmat_ref, qdec_ref, kdec_ref, gn_ref,
                o_ref, state_ref, *, scale):
    h = pl.program_id(1)

    @pl.when(pl.program_id(2) == 0)
    def _():
        state_ref[...] = jnp.zeros_like(state_ref)

    cos, sin = cos_ref[...], sin_ref[...]
    qr = _rope(q_ref[...].astype(F32), cos, sin) * scale
    kr = _rope(k_ref[...].astype(F32), cos, sin)
    v = v_ref[...]
    sc = lax.dot_general(qr.astype(BF16), kr.astype(BF16), (((1,), (1,)), ((), ())),
                         preferred_element_type=F32) * dmat_ref[0]
    o = jnp.dot(sc.astype(BF16), v, preferred_element_type=F32)
    o = o + jnp.dot((qr * qdec_ref[0]).astype(BF16), state_ref[...].astype(BF16), preferred_element_type=F32)
    kd = (kr * kdec_ref[0]).astype(BF16)
    state_ref[...] = state_ref[...] * dec_ref[h] + lax.dot_general(
        kd, v, (((0,), (0,)), ((), ())), preferred_element_type=F32)
    mu = jnp.mean(o, axis=-1, keepdims=True)
    d = o - mu
    var = jnp.mean(d * d, axis=-1, keepdims=True)
    on = d * lax.rsqrt(var + LN_EPS) * gn_ref[...]
    g = g_ref[...].astype(F32)
    o_ref[...] = (g * jax.nn.sigmoid(g) * on).astype(BF16)


def _retention_tables(seq, dk, blk):
    pos = jnp.arange(seq, dtype=F32)
    inv_freq = ROPE_BASE ** (-jnp.arange(0, dk, 2, dtype=F32) / dk)
    ang = pos[:, None] * inv_freq[None, :]
    log_gamma = jnp.log(1.0 - jnp.exp2(-5.0 - jnp.arange(RET_HEADS, dtype=F32)))
    n = jnp.arange(blk, dtype=F32)
    dist = jnp.abs(n[:, None] - n[None, :])
    chunk = jnp.arange(blk) // CHUNK
    visible = (chunk[None, :] <= chunk[:, None]).astype(F32)
    dmat = jnp.exp(log_gamma[:, None, None] * dist) * visible
    qdec = jnp.broadcast_to(jnp.exp(log_gamma[:, None] * (n + 1.0))[:, :, None], (RET_HEADS, blk, dk))
    kdec = jnp.broadcast_to(jnp.exp(log_gamma[:, None] * (blk - 1.0 - n))[:, :, None], (RET_HEADS, blk, dk))
    dec = jnp.exp(log_gamma * blk)
    return jnp.cos(ang), jnp.sin(ang), dmat, qdec, kdec, dec


def _retention_core(proj, gn_g, batch, seq, d_model):
    dk = d_model // RET_HEADS
    dv = 2 * d_model // RET_HEADS
    blk = min(RET_BLOCK, seq)
    nblk = seq // blk
    cos, sin, dmat, qdec, kdec, dec = _retention_tables(seq, dk, blk)
    k_col0 = RET_HEADS
    v_col0 = 2 * RET_HEADS * dk // dv
    g_col0 = v_col0 + RET_HEADS
    head_tab = lambda b, h, s, dec: (h, 0, 0)
    grid_spec = pltpu.PrefetchScalarGridSpec(
        num_scalar_prefetch=1,
        grid=(batch, RET_HEADS, nblk),
        in_specs=[
            pl.BlockSpec((blk, dk), lambda b, h, s, dec: (b * nblk + s, h)),
            pl.BlockSpec((blk, dk), lambda b, h, s, dec: (b * nblk + s, k_col0 + h)),
            pl.BlockSpec((blk, dv), lambda b, h, s, dec: (b * nblk + s, v_col0 + h)),
            pl.BlockSpec((blk, dv), lambda b, h, s, dec: (b * nblk + s, g_col0 + h)),
            pl.BlockSpec((blk, dk // 2), lambda b, h, s, dec: (s, 0)),
            pl.BlockSpec((blk, dk // 2), lambda b, h, s, dec: (s, 0)),
            pl.BlockSpec((1, blk, blk), head_tab),
            pl.BlockSpec((1, blk, dk), head_tab),
            pl.BlockSpec((1, blk, dk), head_tab),
            pl.BlockSpec((1, dv), lambda b, h, s, dec: (0, h)),
        ],
        out_specs=pl.BlockSpec((blk, dv), lambda b, h, s, dec: (b * nblk + s, h)),
        scratch_shapes=[pltpu.VMEM((dk, dv), F32)],
    )
    return pl.pallas_call(
        functools.partial(_ret_kernel, scale=dk ** -0.5),
        grid_spec=grid_spec,
        out_shape=jax.ShapeDtypeStruct((batch * seq, RET_HEADS * dv), BF16),
        compiler_params=_params("parallel", "parallel", "arbitrary"),
        name="retention",
    )(dec, proj, proj, proj, proj, cos, sin, dmat, qdec, kdec, gn_g.reshape(1, -1))


def _sb_kernel(q_ref, k_ref, v_ref, o_ref, *, scale, dh):
    tq = q_ref.shape[0]
    n_heads = q_ref.shape[1] // dh
    tk = min(SB_TK, k_ref.shape[0])
    qi = pl.program_id(2)
    t_idx = qi * tq + lax.broadcasted_iota(jnp.int32, (tq, tk), 0)
    s_loc = lax.broadcasted_iota(jnp.int32, (tq, tk), 1)
    later = (lax.broadcasted_iota(jnp.int32, (tk, tk), 0) > lax.broadcasted_iota(jnp.int32, (tk, tk), 1)).astype(BF16)
    n_kb = (qi + 1) * tq // tk

    def one_head(q, kblk, vblk, causal, c, acc):
        z = lax.dot_general(q, kblk, (((1,), (1,)), ((), ())), preferred_element_type=F32) * scale
        log_beta = jnp.minimum(z, 0.0) - jnp.log(1.0 + jnp.exp(-jnp.abs(z)))
        log_1m = jnp.where(causal, log_beta - z, 0.0)
        hi = log_1m.astype(BF16)
        lo = (log_1m - hi.astype(F32)).astype(BF16)
        tail = (jnp.dot(hi, later, preferred_element_type=F32)
                + jnp.dot(lo, later, preferred_element_type=F32) + c)
        w = jnp.where(causal, jnp.exp(log_beta + tail), 0.0)
        acc = acc + jnp.dot(w.astype(BF16), vblk, preferred_element_type=F32)
        return c + jnp.sum(log_1m, axis=1, keepdims=True), acc

    def body(carry):
        it, _, state = carry
        kb = n_kb - 1 - it
        off = pl.multiple_of(kb * tk, tk)
        causal = (kb * tk + s_loc) < t_idx
        new_state = []
        c_max = None
        for u, (c, acc) in enumerate(state):
            lanes = slice(u * dh, (u + 1) * dh)
            c, acc = one_head(q_ref[:, lanes], k_ref[pl.ds(off, tk), lanes], v_ref[pl.ds(off, tk), lanes],
                              causal, c, acc)
            new_state.append((c, acc))
            c_max = jnp.max(c) if c_max is None else jnp.maximum(c_max, jnp.max(c))
        return it + 1, c_max, tuple(new_state)

    def more(carry):
        it, c_max, _ = carry
        return jnp.logical_and(it < n_kb, c_max > SB_EXP_ZERO)

    state = tuple((jnp.zeros((tq, 1), F32), jnp.zeros((tq, dh), F32)) for _ in range(n_heads))
    _, _, state = lax.while_loop(more, body, (jnp.int32(0), jnp.float32(0.0), state))
    for u, (_, acc) in enumerate(state):
        o_ref[:, u * dh:(u + 1) * dh] = acc.astype(o_ref.dtype)


def _stick_breaking_core(q, kv, batch, seq, d_model):
    dh = d_model // SB_HEADS
    tq = min(SB_TQ, seq)
    nq = seq // tq
    wide = SB_HEADS_PER_STEP * dh
    n_groups = SB_HEADS // SB_HEADS_PER_STEP
    return pl.pallas_call(
        functools.partial(_sb_kernel, scale=dh ** -0.5, dh=dh),
        grid=(batch, n_groups, nq),
        in_specs=[pl.BlockSpec((tq, wide), lambda b, h, i: (b * nq + i, h)),
                  pl.BlockSpec((seq, wide), lambda b, h, i: (b, h)),
                  pl.BlockSpec((seq, wide), lambda b, h, i: (b, n_groups + h))],
        out_specs=pl.BlockSpec((tq, wide), lambda b, h, i: (b * nq + i, h)),
        out_shape=jax.ShapeDtypeStruct((batch * seq, d_model), BF16),
        compiler_params=_params("parallel", "parallel", "parallel"),
        name="stick_breaking",
    )(q, kv, kv)


def _peer_score_kernel(xT_ref, wqT_ref, keys_ref, sT_ref):
    qT = jnp.dot(wqT_ref[...], xT_ref[...], preferred_element_type=F32)
    dq = keys_ref.shape[2]
    for g in range(keys_ref.shape[0]):
        sT_ref[g] = jnp.dot(keys_ref[g], qT[g * dq:(g + 1) * dq].astype(BF16), preferred_element_type=F32)


def _peer_scores(xT, wqT, keys):
    d, t = xT.shape
    groups, nk, dq = keys.shape
    tm = min(PEER_SCORE_TM, t)
    return pl.pallas_call(
        _peer_score_kernel,
        grid=(t // tm,),
        in_specs=[pl.BlockSpec((d, tm), lambda i: (0, i)),
                  pl.BlockSpec(wqT.shape, lambda i: (0, 0)),
                  pl.BlockSpec(keys.shape, lambda i: (0, 0, 0))],
        out_specs=pl.BlockSpec((groups, nk, tm), lambda i: (0, 0, i)),
        out_shape=jax.ShapeDtypeStruct((groups, nk, t), F32),
        compiler_params=_params("parallel"),
        name="peer_scores",
    )(xT, wqT, keys)


def _row_ids(shape):
    return lax.broadcasted_iota(jnp.int32, shape, 0).astype(F32)


def _argmax_mask(s, row, exact):
    m = jnp.max(s, axis=0, keepdims=True)
    if not exact:
        return m, s == m
    idx = jnp.min(jnp.where(s == m, row, float(s.shape[0])), axis=0, keepdims=True)
    return m, row == idx


def _topk_rank(s, vals_ref, exact):
    row = _row_ids(s.shape)
    rank = jnp.full(s.shape, float(PEER_TOPK), F32)
    for k in range(PEER_TOPK):
        m, sel = _argmax_mask(s, row, exact)
        rank = jnp.where(sel, float(k), rank)
        s = jnp.where(sel, -jnp.inf, s)
        vals_ref[k:k + 1, :] = m
    return rank


def _peer_topk_kernel(sT_ref, r2_ref, e2_ref, n_ref, a_ref, v1s_ref, v2s_ref, cs_ref, sels_ref):
    tm = sT_ref.shape[2]
    crow = _row_ids((_CAND_ROWS, tm))

    def rank_head(h, slot, exact):
        v1_ref, v2_ref, c_ref, sel_ref = (ref.at[slot] for ref in (v1s_ref, v2s_ref, cs_ref, sels_ref))
        s1 = sT_ref[2 * h]
        s2 = sT_ref[2 * h + 1]
        rank1 = _topk_rank(s1, v1_ref, exact)
        rank2 = _topk_rank(s2, v2_ref, exact)
        c_ref[...] = jnp.full(c_ref.shape, -jnp.inf, F32)
        for r, (p, q) in enumerate(_CAND):
            c_ref[r:r + 1, :] = v1_ref[p:p + 1, :] + v2_ref[q:q + 1, :]
        c = c_ref[...]
        chosen = jnp.zeros(c.shape, F32)
        m0 = v1_ref[0:1, :] + v2_ref[0:1, :]
        z = jnp.zeros((1, tm), F32)
        for _ in range(PEER_TOPK):
            m, sel = _argmax_mask(c, crow, exact)
            chosen = jnp.where(sel, 1.0, chosen)
            c = jnp.where(sel, -jnp.inf, c)
            z = z + jnp.exp(m - m0)
        sel_ref[...] = chosen
        n = jnp.zeros(s1.shape, F32)
        r0 = 0
        for p in range(PEER_TOPK):
            width = sum(1 for (pp, _) in _CAND if pp == p)
            n_p = jnp.sum(sel_ref[r0:r0 + width, :], axis=0, keepdims=True)
            n = jnp.where(rank1 == float(p), n_p, n)
            r0 += width
        r2_ref[h] = rank2.astype(r2_ref.dtype)
        e2_ref[h] = jnp.exp(s2 - v2_ref[0:1, :]).astype(e2_ref.dtype)
        n_ref[h] = n
        a_ref[h] = jnp.exp(s1 - v1_ref[0:1, :]) / z
        taken = jnp.maximum(
            jnp.maximum(jnp.sum(jnp.where(rank1 < PEER_TOPK, 1.0, 0.0), axis=0, keepdims=True),
                        jnp.sum(jnp.where(rank2 < PEER_TOPK, 1.0, 0.0), axis=0, keepdims=True)),
            jnp.sum(chosen, axis=0, keepdims=True))
        return jnp.max(taken)

    def head_group(hg, carry):
        heads = [hg * PEER_TOPK_HEADS_PER_STEP + u for u in range(PEER_TOPK_HEADS_PER_STEP)]
        most_taken = [rank_head(h, u, exact=False) for u, h in enumerate(heads)]
        for u, h in enumerate(heads):
            @pl.when(most_taken[u] > PEER_TOPK)
            def _():
                rank_head(h, u, exact=True)

        return carry

    lax.fori_loop(0, r2_ref.shape[0] // PEER_TOPK_HEADS_PER_STEP, head_group, 0)


def _peer_topk(sT):
    groups, nk, t = sT.shape
    heads = groups // 2
    tm = min(PEER_TOPK_TM, t)
    out = pl.BlockSpec((heads, nk, tm), lambda i: (0, 0, i))
    wide = jax.ShapeDtypeStruct((heads, nk, t), F32)
    narrow = jax.ShapeDtypeStruct((heads, nk, t), BF16)
    return pl.pallas_call(
        _peer_topk_kernel,
        grid=(t // tm,),
        in_specs=[pl.BlockSpec((groups, nk, tm), lambda i: (0, 0, i))],
        out_specs=[out, out, out, out],
        out_shape=[narrow, narrow, wide, wide],
        scratch_shapes=[pltpu.VMEM((PEER_TOPK_HEADS_PER_STEP, rows, tm), F32)
                        for rows in (PEER_TOPK, PEER_TOPK, _CAND_ROWS, _CAND_ROWS)],
        compiler_params=_params("parallel"),
        name="peer_topk",
    )(sT)


def _peer_dense_kernel(xT_ref, u_ref, vT_ref, r2_ref, e2_ref, n_ref, a_ref, y_ref, act_ref):
    e = pl.program_id(1)
    heads, nk, _ = r2_ref.shape
    ni = u_ref.shape[0] // nk

    @pl.when(e == 0)
    def _():
        y_ref[...] = jnp.zeros_like(y_ref)

    zero = jnp.zeros((), e2_ref.dtype)
    group = ni // PEER_GROUPS
    for gi in range(PEER_GROUPS):
        rows = slice(gi * group * nk, (gi + 1) * group * nk)
        hT = jnp.dot(u_ref[rows, :], xT_ref[...], preferred_element_type=F32)
        for ii in range(group):
            i = e * ni + gi * group + ii
            w = None
            for h in range(heads):
                n = n_ref[h, pl.ds(i, 1), :].astype(r2_ref.dtype)
                a = a_ref[h, pl.ds(i, 1), :].astype(e2_ref.dtype)
                term = jnp.where(r2_ref[h] < n, e2_ref[h], zero) * a
                w = term if w is None else w + term
            g = jax.nn.gelu(hT[ii * nk:(ii + 1) * nk].astype(w.dtype))
            act_ref[(gi * group + ii) * nk:(gi * group + ii + 1) * nk, :] = g * w
        y_ref[...] += jnp.dot(vT_ref[:, rows], act_ref[rows, :], preferred_element_type=F32)


def _peer_dense(xT, u, vT, r2, e2, n, a):
    d, t = xT.shape
    n_exp = u.shape[0]
    heads, nk, _ = r2.shape
    tm = min(PEER_TM, t)
    te = PEER_NI * nk
    sel = pl.BlockSpec((heads, nk, tm), lambda i, e: (0, 0, i))
    return pl.pallas_call(
        _peer_dense_kernel,
        grid=(t // tm, n_exp // te),
        in_specs=[pl.BlockSpec((d, tm), lambda i, e: (0, i)),
                  pl.BlockSpec((te, d), lambda i, e: (e, 0)),
                  pl.BlockSpec((d, te), lambda i, e: (0, e)),
                  sel, sel, sel, sel],
        out_specs=pl.BlockSpec((d, tm), lambda i, e: (0, i)),
        out_shape=jax.ShapeDtypeStruct((d, t), F32),
        scratch_shapes=[pltpu.VMEM((te, tm), BF16)],
        compiler_params=_params("parallel", "arbitrary"),
        name="peer_dense",
    )(xT, u, vT, r2, e2, n, a)


def _peer(xb, wq, sub_keys, u_tab, v_tab):
    xT = xb.T
    keys = sub_keys.reshape(-1, sub_keys.shape[-2], sub_keys.shape[-1]).astype(BF16)
    sT = _peer_scores(xT, wq.T.astype(BF16), keys)
    r2, e2, n, a = _peer_topk(sT)
    return _peer_dense(xT, u_tab.astype(BF16), v_tab.T.astype(BF16), r2, e2, n, a)


def kernel(x, ret_w_in, ret_gn_g, ret_w_out, kv_w, sb_wq, sb_w_out, peer_wq, peer_sub_keys, peer_u, peer_v,
           ln_g, ln_b):
    batch, seq, d_model = x.shape
    depth = ln_g.shape[0]
    n_a = ret_w_in.shape[0]
    alpha = (2.0 * depth) ** 0.25
    xf = x.reshape(batch * seq, d_model)
    xb = xf.astype(BF16)
    kv = None
    for l in range(depth):
        if l < n_a:
            proj = _matmul(xb, ret_w_in[l].astype(BF16), BF16)
            o = _retention_core(proj, ret_gn_g[l], batch, seq, d_model)
            w_out = ret_w_out[l]
        else:
            q = _matmul(xb, sb_wq[l - n_a].astype(BF16), BF16)
            o = _stick_breaking_core(q, kv, batch, seq, d_model)
            w_out = sb_w_out[l - n_a]
        xf, xb = _matmul_ln(o, w_out.astype(BF16), xf, ln_g[l, 0], ln_b[l, 0], alpha)
        yT = _peer(xb, peer_wq[l], peer_sub_keys[l], peer_u[l], peer_v[l])
        xf, xb = _ln_res_t(xf, yT, ln_g[l, 1], ln_b[l, 1], alpha)
        if l == n_a - 1:
            kv = _matmul(xb, kv_w.astype(BF16), BF16)
    return xf.reshape(batch, seq, d_model)
```
